```python
import math
import jax, jax.numpy as jnp
from jax import lax
import numpy as np

D_MODEL = 1024
BATCH = 8
SEQ = 2048
DEPTH = 4

GRID_W = 64
CTX_LEN = 256
HEAD_DIM = 64
Q_BLOCK = 128
ROPE_THETA = 10000.0
RMS_EPS = 1e-6
N_MOD = 9
N_BRANCH = 4
BRANCH_W = D_MODEL // 2
FFN_DIM = 256 * ((8 * D_MODEL // 3 + 255) // 256)
GQA_Q_HEADS = BRANCH_W // HEAD_DIM
GQA_KV_HEADS = 2
GQA_GROUP = GQA_Q_HEADS // GQA_KV_HEADS
GQA_SCALE = HEAD_DIM ** -0.5
HY_W = BRANCH_W
HY_ORDER = 2
HY_IN = (HY_ORDER + 1) * HY_W
HY_CONV = 3
HY_BANDS = 16
HY_EMB = 1 + 2 * HY_BANDS
HY_HIDDEN = 64
HY_DECAY_TARGET = 1e-2
HY_FAST_DECAY = 0.3
HY_SLOW_DECAY = 1.5
HY_MIN_DECAY = math.log(HY_DECAY_TARGET) / HY_SLOW_DECAY
HY_MAX_DECAY = math.log(HY_DECAY_TARGET) / HY_FAST_DECAY
NA_HEADS = BRANCH_W // HEAD_DIM
NA_ROWS = 8
NA_COLS = 16
NA_SCALE = HEAD_DIM ** -0.5
MLA_HEADS = 8
MLA_NOPE = 64
MLA_ROPE = 32
MLA_V = BRANCH_W // MLA_HEADS
MLA_Q_RANK = 3 * D_MODEL // 8
MLA_KV_RANK = D_MODEL // 4
MLA_SCALE = (MLA_NOPE + MLA_ROPE) ** -0.5
SPLIT_SIZES = (GQA_Q_HEADS * HEAD_DIM, GQA_KV_HEADS * HEAD_DIM, GQA_KV_HEADS * HEAD_DIM,
               HY_IN,
               NA_HEADS * HEAD_DIM, NA_HEADS * HEAD_DIM, NA_HEADS * HEAD_DIM,
               MLA_Q_RANK, MLA_KV_RANK, MLA_ROPE,
               N_BRANCH * D_MODEL)
SPLIT_OFFSETS = [int(v) for v in np.cumsum(SPLIT_SIZES)[:-1]]
IN_COLS = int(sum(SPLIT_SIZES))

kernel_name = 'hybrid_gated_multimixer_dit'


def rms_norm(x, gain=None):
    xf = x.astype(jnp.float32)
    y = (xf * lax.rsqrt(jnp.mean(xf * xf, axis=-1, keepdims=True) + RMS_EPS)).astype(x.dtype)
    return y if gain is None else y * gain


def modulate(x, shift, scale):
    return x * (1 + scale) + shift


def adaln(cond, w, b):
    mod = cond @ w + b
    return jnp.split(mod[:, None, :], N_MOD, axis=-1)


def swiglu(x, w_up, w_down):
    a, g = jnp.split(x @ w_up, 2, axis=-1)
    return (jax.nn.silu(a) * g) @ w_down


def split_heads(x, n_heads):
    return x.reshape(*x.shape[:-1], n_heads, x.shape[-1] // n_heads)


def rope_1d(x, pos):
    half = x.shape[-1] // 2
    freqs = ROPE_THETA ** (-jnp.arange(half, dtype=jnp.float32) / half)
    ang = pos.astype(jnp.float32)[:, None] * freqs
    cos = jnp.cos(ang)[:, None, :].astype(x.dtype)
    sin = jnp.sin(ang)[:, None, :].astype(x.dtype)
    x1, x2 = x[..., :half], x[..., half:]
    return jnp.concatenate([x1 * cos - x2 * sin, x2 * cos + x1 * sin], axis=-1)


def rope_2d(x, rows, cols):
    half = x.shape[-1] // 2
    return jnp.concatenate([rope_1d(x[..., :half], rows), rope_1d(x[..., half:], cols)], axis=-1)


def sweep_query_blocks(fn, qs):
    b, s = qs[0].shape[:2]
    blk = min(Q_BLOCK, s)
    nb = s // blk
    xs = tuple(a.reshape(b, nb, blk, *a.shape[2:]).swapaxes(0, 1) for a in qs)
    out = lax.map(lambda blocks: fn(*blocks), xs)
    return out.swapaxes(0, 1).reshape(b, s, *out.shape[3:])


def attend_gqa(q, k, v, scale):
    s = jnp.einsum('bqhgd,bkhd->bhgqk', q, k).astype(jnp.float32) * scale
    p = jax.nn.softmax(s, axis=-1).astype(v.dtype)
    return jnp.einsum('bhgqk,bkhd->bqhgd', p, v)


def gqa_branch(q_l, k_l, v_l, q_c, k_c, v_c, q_gain, k_gain, rows, cols, with_ctx):
    b, s = q_l.shape[:2]
    q = rope_2d(rms_norm(split_heads(q_l, GQA_Q_HEADS), q_gain), rows, cols)
    q = q.reshape(b, s, GQA_KV_HEADS, GQA_GROUP, HEAD_DIM)
    k = rope_2d(rms_norm(split_heads(k_l, GQA_KV_HEADS), k_gain), rows, cols)
    kc = rms_norm(split_heads(k_c, GQA_KV_HEADS), k_gain)
    vc = split_heads(v_c, GQA_KV_HEADS)
    k_all = jnp.concatenate([k, kc], axis=1)
    v_all = jnp.concatenate([split_heads(v_l, GQA_KV_HEADS), vc], axis=1)
    o = sweep_query_blocks(lambda qb: attend_gqa(qb, k_all, v_all, GQA_SCALE), (q,)).reshape(b, s, -1)
    if not with_ctx:
        return o, None
    n_c = q_c.shape[1]
    qc = rms_norm(split_heads(q_c, GQA_Q_HEADS), q_gain).reshape(b, n_c, GQA_KV_HEADS, GQA_GROUP, HEAD_DIM)
    return o, attend_gqa(qc, kc, vc, GQA_SCALE).reshape(b, n_c, -1)


def short_conv(u, w, bias):
    ch = u.shape[-1]
    y = lax.conv_general_dilated(u, w[:, None, :], window_strides=(1,),
                                 padding=[(HY_CONV // 2, HY_CONV // 2)],
                                 dimension_numbers=('NWC', 'WIO', 'NWC'),
                                 feature_group_count=ch)
    return y + bias


def hyena_filter(length, f_w1, f_b, f_freq, f_w_mid, f_w_out):
    t = jnp.linspace(0.0, 1.0, length, dtype=jnp.float32)[:, None]
    w = (2.0 * math.pi / length) * jnp.arange(length, dtype=jnp.float32)[:, None]
    f = jnp.linspace(1e-4, HY_BANDS - 1, HY_BANDS, dtype=jnp.float32)[None, :]
    z = jnp.concatenate([t, jnp.cos(f * w), -jnp.sin(f * w)], axis=-1).astype(f_w1.dtype)
    hdn = jnp.sin(f_freq[0] * (z @ f_w1 + f_b[0]))
    hdn = jnp.sin(f_freq[1] * (hdn @ f_w_mid[0] + f_b[1]))
    hdn = jnp.sin(f_freq[2] * (hdn @ f_w_mid[1] + f_b[2]))
    filt = (hdn @ f_w_out).astype(jnp.float32).reshape(length, 2, HY_W)
    deltas = jnp.abs(jnp.linspace(HY_MIN_DECAY, HY_MAX_DECAY, HY_W, dtype=jnp.float32))
    filt = filt * jnp.exp(-t * deltas)[:, None, :]
    k_full = jnp.concatenate([filt[:, 0], jnp.zeros((1, HY_W), jnp.float32), filt[:0:-1, 1]], axis=0)
    return k_full / jnp.sum(jnp.abs(k_full), axis=0, keepdims=True)


def fft_conv(v, k_full):
    length = v.shape[1]
    n = 2 * length
    vf = jnp.fft.rfft(v.astype(jnp.float32), n=n, axis=1)
    kf = jnp.fft.rfft(k_full, n=n, axis=0)
    y = jnp.fft.irfft(vf * kf[None], n=n, axis=1)[:, :length]
    return y.astype(v.dtype)


def hyena_mix(u, conv_w, conv_b, f_w1, f_b, f_freq, f_w_mid, f_w_out, skip):
    u = short_conv(u, conv_w, conv_b)
    x0, x1, v = jnp.split(u, HY_ORDER + 1, axis=-1)
    k_full = hyena_filter(u.shape[1], f_w1, f_b, f_freq, f_w_mid, f_w_out)
    v = v * x1
    y = fft_conv(v, k_full) + v * skip
    return y * x0


def na_latent(q, k, v, k_ctx, v_ctx, rpb):
    b, s, nh, hd = q.shape
    n_rows = s // GRID_W
    kr = min(NA_ROWS, n_rows)
    n_loc = kr * GRID_W
    qc = jnp.arange(GRID_W)[:, None]
    kc = jnp.arange(GRID_W)[None, :]
    col_start = jnp.clip(qc - NA_COLS // 2, 0, GRID_W - NA_COLS)
    col_ok = (kc >= col_start) & (kc < col_start + NA_COLS)
    col_idx = jnp.clip(kc - qc + NA_COLS - 1, 0, 2 * NA_COLS - 2)
    q_rows = q.reshape(b, n_rows, GRID_W, nh, hd).swapaxes(0, 1)

    def one_row(args):
        q_r, r = args
        start = jnp.clip(r - kr // 2, 0, n_rows - kr)
        k_loc = lax.dynamic_slice_in_dim(k, start * GRID_W, n_loc, axis=1)
        v_loc = lax.dynamic_slice_in_dim(v, start * GRID_W, n_loc, axis=1)
        row_idx = start + jnp.arange(kr) - r + NA_ROWS - 1
        bias = rpb[:, row_idx][:, :, col_idx].astype(jnp.float32)
        bias = jnp.where(col_ok, bias, -jnp.inf).transpose(0, 2, 1, 3).reshape(nh, GRID_W, n_loc)
        s_loc = jnp.einsum('bqhd,bkhd->bhqk', q_r, k_loc).astype(jnp.float32) * NA_SCALE + bias
        s_ctx = jnp.einsum('bqhd,bkhd->bhqk', q_r, k_ctx).astype(jnp.float32) * NA_SCALE
        p = jax.nn.softmax(jnp.concatenate([s_loc, s_ctx], axis=-1), axis=-1).astype(v.dtype)
        return (jnp.einsum('bhqk,bkhd->bqhd', p[..., :n_loc], v_loc)
                + jnp.einsum('bhqk,bkhd->bqhd', p[..., n_loc:], v_ctx))

    out = lax.map(one_row, (q_rows, jnp.arange(n_rows)))
    return out.swapaxes(0, 1).reshape(b, s, nh * hd)


def na_branch(q_l, k_l, v_l, q_c, k_c, v_c, rpb, with_ctx):
    kc = split_heads(k_c, NA_HEADS)
    vc = split_heads(v_c, NA_HEADS)
    o = na_latent(split_heads(q_l, NA_HEADS), split_heads(k_l, NA_HEADS), split_heads(v_l, NA_HEADS), kc, vc, rpb)
    if not with_ctx:
        return o, None
    b, n_c = q_c.shape[:2]
    qc = split_heads(q_c, NA_HEADS)[:, :, :, None, :]
    return o, attend_gqa(qc, kc, vc, NA_SCALE).reshape(b, n_c, -1)


def attend_mla(q_nope, q_rope, k_nope, k_rope, v):
    s = (jnp.einsum('bqhd,bkhd->bhqk', q_nope, k_nope)
         + jnp.einsum('bqhr,bkr->bhqk', q_rope, k_rope)).astype(jnp.float32) * MLA_SCALE
    p = jax.nn.softmax(s, axis=-1).astype(v.dtype)
    return jnp.einsum('bhqk,bkhd->bqhd', p, v)


def mla_branch(qa, kva, kr, qa_c, kva_c, kr_c, q_gain, kv_gain, wq_b, wkv_b, rows, cols, with_ctx):
    def queries(a):
        qh = split_heads(rms_norm(a, q_gain) @ wq_b, MLA_HEADS)
        return qh[..., :MLA_NOPE], qh[..., MLA_NOPE:]

    def keys_values(a):
        kvh = split_heads(rms_norm(a, kv_gain) @ wkv_b, MLA_HEADS)
        return kvh[..., :MLA_NOPE], kvh[..., MLA_NOPE:]

    b, s = qa.shape[:2]
    q_nope, q_rope = queries(qa)
    q_rope = rope_2d(q_rope, rows, cols)
    k_nope, v = keys_values(kva)
    k_rope = rope_2d(kr[:, :, None, :], rows, cols)[:, :, 0]
    kn_c, v_c = keys_values(kva_c)
    kn_all = jnp.concatenate([k_nope, kn_c], axis=1)
    kr_all = jnp.concatenate([k_rope, kr_c], axis=1)
    v_all = jnp.concatenate([v, v_c], axis=1)
    o = sweep_query_blocks(lambda qn, qr: attend_mla(qn, qr, kn_all, kr_all, v_all), (q_nope, q_rope)).reshape(b, s, -1)
    if not with_ctx:
        return o, None
    qn_c, qr_c = queries(qa_c)
    return o, attend_mla(qn_c, qr_c, kn_c, kr_c, v_c).reshape(b, qa_c.shape[1], -1)


def merge_branches(outs, gate_logits, w_branch, w_out):
    b, t = gate_logits.shape[:2]
    gates = jax.nn.sigmoid(gate_logits.reshape(b, t, N_BRANCH, D_MODEL))
    merged = gates[:, :, 0] * (outs[0] @ w_branch[0])
    for i in range(1, N_BRANCH):
        merged = merged + gates[:, :, i] * (outs[i] @ w_branch[i])
    return merged @ w_out


def token_mix(n, nc, rows, cols, w_in, gqa_p, hy_p, na_rpb, mla_p, w_branch, w_out, with_ctx):
    (gq, gk, gv, hy_u, na_q, na_k, na_v, m_qa, m_kva, m_kr, gate) = jnp.split(n @ w_in, SPLIT_OFFSETS, axis=-1)
    (gq_c, gk_c, gv_c, hy_u_c, na_q_c, na_k_c, na_v_c, m_qa_c, m_kva_c, m_kr_c, gate_c) = jnp.split(nc @ w_in, SPLIT_OFFSETS, axis=-1)
    o_gqa, c_gqa = gqa_branch(gq, gk, gv, gq_c, gk_c, gv_c, gqa_p[0], gqa_p[1], rows, cols, with_ctx)
    o_hy = hyena_mix(hy_u, *hy_p)
    o_na, c_na = na_branch(na_q, na_k, na_v, na_q_c, na_k_c, na_v_c, na_rpb, with_ctx)
    o_mla, c_mla = mla_branch(m_qa, m_kva, m_kr, m_qa_c, m_kva_c, m_kr_c, *mla_p, rows, cols, with_ctx)
    out = merge_branches((o_gqa, o_hy, o_na, o_mla), gate, w_branch, w_out)
    if not with_ctx:
        return out, None
    c_hy = hyena_mix(hy_u_c, *hy_p)
    out_c = merge_branches((c_gqa, c_hy, c_na, c_mla), gate_c, w_branch, w_out)
    return out, out_c


def setup_inputs(seed: int = 0) -> dict:
    key = jax.random.key(seed)
    ks = jax.random.split(key, 32)
    f32 = jnp.float32

    def nrm(k, shape, scale):
        return jax.random.normal(k, shape, f32) * scale

    D = D_MODEL
    return {
        'x': nrm(ks[0], (BATCH, SEQ, D), 1.0),
        'c': nrm(ks[1], (BATCH, D), 1.0),
        'ctx': nrm(ks[2], (BATCH, CTX_LEN, D), 1.0),
        'c_ctx': nrm(ks[3], (D,), 1.0),
        'ada_w': nrm(ks[4], (DEPTH, D, N_MOD * D), D ** -0.5),
        'ada_b': nrm(ks[5], (DEPTH, N_MOD * D), 0.02),
        'ffn1_up': nrm(ks[6], (DEPTH, D, 2 * FFN_DIM), D ** -0.5),
        'ffn1_down': nrm(ks[7], (DEPTH, FFN_DIM, D), FFN_DIM ** -0.5),
        'w_in': nrm(ks[8], (DEPTH, D, IN_COLS), D ** -0.5),
        'gqa_q_norm': 1.0 + nrm(ks[9], (DEPTH, HEAD_DIM), 0.05),
        'gqa_k_norm': 1.0 + nrm(ks[10], (DEPTH, HEAD_DIM), 0.05),
        'hy_conv_w': nrm(ks[11], (DEPTH, HY_CONV, HY_IN), HY_CONV ** -0.5),
        'hy_conv_b': nrm(ks[12], (DEPTH, HY_IN), 0.02),
        'hy_f_w1': nrm(ks[13], (DEPTH, HY_EMB, HY_HIDDEN), HY_EMB ** -0.5),
        'hy_f_b': nrm(ks[14], (DEPTH, 3, HY_HIDDEN), 0.1),
        'hy_f_freq': 1.0 + nrm(ks[15], (DEPTH, 3, HY_HIDDEN), 0.05),
        'hy_f_w_mid': nrm(ks[16], (DEPTH, 2, HY_HIDDEN, HY_HIDDEN), HY_HIDDEN ** -0.5),
        'hy_f_w_out': nrm(ks[17], (DEPTH, HY_HIDDEN, 2 * HY_W), HY_HIDDEN ** -0.5),
        'hy_skip': nrm(ks[18], (DEPTH, HY_W), 0.5),
        'na_rpb': nrm(ks[19], (DEPTH, NA_HEADS, 2 * NA_ROWS - 1, 2 * NA_COLS - 1), 0.1),
        'mla_q_norm': 1.0 + nrm(ks[20], (DEPTH, MLA_Q_RANK), 0.05),
        'mla_kv_norm': 1.0 + nrm(ks[21], (DEPTH, MLA_KV_RANK), 0.05),
        'mla_wq_b': nrm(ks[22], (DEPTH, MLA_Q_RANK, MLA_HEADS * (MLA_NOPE + MLA_ROPE)), MLA_Q_RANK ** -0.5),
        'mla_wkv_b': nrm(ks[23], (DEPTH, MLA_KV_RANK, MLA_HEADS * (MLA_NOPE + MLA_V)), MLA_KV_RANK ** -0.5),
        'w_branch': nrm(ks[24], (DEPTH, N_BRANCH, BRANCH_W, D), BRANCH_W ** -0.5),
        'w_out': nrm(ks[25], (DEPTH, D, D), D ** -0.5),
        'ffn2_up': nrm(ks[26], (DEPTH, D, 2 * FFN_DIM), D ** -0.5),
        'ffn2_down': nrm(ks[27], (DEPTH, FFN_DIM, D), FFN_DIM ** -0.5),
        'final_norm': 1.0 + nrm(ks[28], (D,), 0.05),
    }


def reference(x, c, ctx, c_ctx, ada_w, ada_b, ffn1_up, ffn1_down, w_in, gqa_q_norm, gqa_k_norm,
              hy_conv_w, hy_conv_b, hy_f_w1, hy_f_b, hy_f_freq, hy_f_w_mid, hy_f_w_out, hy_skip,
              na_rpb, mla_q_norm, mla_kv_norm, mla_wq_b, mla_wkv_b, w_branch, w_out,
              ffn2_up, ffn2_down, final_norm):
    s = x.shape[1]
    pos = jnp.arange(s)
    rows = pos // GRID_W
    cols = pos % GRID_W
    cond_lat = jax.nn.silu(c)
    cond_ctx = jax.nn.silu(c_ctx)[None]
    h, hc = x, ctx
    for l in range(DEPTH):
        with_ctx = l < DEPTH - 1
        m = adaln(cond_lat, ada_w[l], ada_b[l])
        mc = adaln(cond_ctx, ada_w[l], ada_b[l])
        h = h + 0.5 * m[2] * swiglu(modulate(rms_norm(h), m[0], m[1]), ffn1_up[l], ffn1_down[l])
        hc = hc + 0.5 * mc[2] * swiglu(modulate(rms_norm(hc), mc[0], mc[1]), ffn1_up[l], ffn1_down[l])
        n = modulate(rms_norm(h), m[3], m[4])
        nc = modulate(rms_norm(hc), mc[3], mc[4])
        hy_p = (hy_conv_w[l], hy_conv_b[l], hy_f_w1[l], hy_f_b[l], hy_f_freq[l], hy_f_w_mid[l], hy_f_w_out[l], hy_skip[l])
        mla_p = (mla_q_norm[l], mla_kv_norm[l], mla_wq_b[l], mla_wkv_b[l])
        mix, mix_c = token_mix(n, nc, rows, cols, w_in[l], (gqa_q_norm[l], gqa_k_norm[l]), hy_p,
                               na_rpb[l], mla_p, w_branch[l], w_out[l], with_ctx)
        h = h + m[5] * mix
        h = h + 0.5 * m[8] * swiglu(modulate(rms_norm(h), m[6], m[7]), ffn2_up[l], ffn2_down[l])
        if with_ctx:
            hc = hc + mc[5] * mix_c
            hc = hc + 0.5 * mc[8] * swiglu(modulate(rms_norm(hc), mc[6], mc[7]), ffn2_up[l], ffn2_down[l])
    return rms_norm(h, final_norm)
```

```python
import functools
import math

import numpy as np
import jax
import jax.numpy as jnp
from jax import lax
from jax.experimental import pallas as pl
from jax.experimental.pallas import tpu as pltpu

f32 = jnp.float32
bf16 = jnp.bfloat16

D = 1024
S = 2048
DEPTH = 4
GRID_W = 64
NCTX = 256
HEAD = 64
ROPE_THETA = 10000.0
RMS_EPS = 1e-6
N_MOD = 9
F = 2816
BW = 512
GQA_H, GQA_KVH, GQA_GROUP = 8, 2, 4
NA_H, NA_ROWS, NA_COLS = 8, 8, 16
MLA_H, MLA_NOPE, MLA_ROPE, MLA_V = 8, 64, 32, 64
MLA_QR, MLA_KVR = 384, 256
MLA_SCALE = (MLA_NOPE + MLA_ROPE) ** -0.5
HY_W = 512
HY_BANDS = 16
HY_EMB = 1 + 2 * HY_BANDS
HY_HID = 64
HY_MIN_DECAY = math.log(1e-2) / 1.5
HY_MAX_DECAY = math.log(1e-2) / 0.3
NEG = -1e30

C_GATE = 0
C_HY = 4096
C_GQ = 5632
C_NQ, C_NK, C_NV = 6144, 6656, 7168
C_MKVA = 7680
C_GK, C_GV = 7936, 8064
C_MQA = 8192
C_MKR = 8704
PW = 8832

TQ = 256
TM = 512
VMEM_BIG = 56 * 1024 * 1024


def _cp(sem, vmem=None):
    return pltpu.CompilerParams(dimension_semantics=sem, vmem_limit_bytes=vmem)


def _resident(shape):
    nd = len(shape)
    return pl.BlockSpec(shape, lambda *_: (0,) * nd, pipeline_mode=pl.Buffered(1))


def _dot(a, b):
    return jnp.dot(a, b, preferred_element_type=f32)


def _dot_t(a, b):
    return lax.dot_general(a, b, (((1,), (1,)), ((), ())), preferred_element_type=f32)


def _rms(x):
    return x * lax.rsqrt(jnp.mean(x * x, axis=-1, keepdims=True) + RMS_EPS)


def _group_of_tile(i, n_batch):
    return jnp.minimum((i * TM) // S, n_batch)


def _mod_spec(n_batch):
    return pl.BlockSpec((None, 1, D), lambda i: (_group_of_tile(i, n_batch), 0, 0))


def _ada_body(c_ref, w_ref, b_ref, o_ref):
    x = c_ref[...]
    xs = (x * jax.nn.sigmoid(x)).astype(bf16)
    o_ref[...] = _dot(xs, w_ref[...].astype(bf16)) + b_ref[...]


def _adaln(cond, ada_w, ada_b):
    tn = 1024
    return pl.pallas_call(
        _ada_body,
        out_shape=jax.ShapeDtypeStruct((DEPTH, 16, N_MOD * D), f32),
        grid=(DEPTH, N_MOD * D // tn),
        in_specs=[pl.BlockSpec((16, D), lambda l, j: (0, 0)),
                  pl.BlockSpec((None, D, tn), lambda l, j: (l, 0, j)),
                  pl.BlockSpec((None, 1, tn), lambda l, j: (l, 0, j))],
        out_specs=pl.BlockSpec((None, 16, tn), lambda l, j: (l, 0, j)),
        compiler_params=_cp(("parallel", "parallel")),
        name="adaln",
    )(cond, ada_w, ada_b.reshape(DEPTH, 1, N_MOD * D))


def _ffn_body(h_ref, sh_ref, sc_ref, gt_ref, wup_ref, wdn_ref, o_ref):
    h = h_ref[...]
    xm = (_rms(h) * (1.0 + sc_ref[...]) + sh_ref[...]).astype(bf16)
    a = _dot(xm, wup_ref[:, :F])
    g = _dot(xm, wup_ref[:, F:])
    mid = (a * jax.nn.sigmoid(a) * g).astype(bf16)
    y = _dot(mid, wdn_ref[...])
    o_ref[...] = h + (0.5 * gt_ref[...]) * y


def _ffn(h, shift, scale, gate, w_up, w_down, n_rows, n_batch):
    tm = 256
    grp = lambda i: (jnp.minimum((i * tm) // S, n_batch), 0, 0)
    mspec = pl.BlockSpec((None, 1, D), grp)
    return pl.pallas_call(
        _ffn_body,
        out_shape=jax.ShapeDtypeStruct((n_rows, D), f32),
        grid=(n_rows // tm,),
        in_specs=[pl.BlockSpec((tm, D), lambda i: (i, 0)), mspec, mspec, mspec,
                  _resident((D, 2 * F)), _resident((F, D))],
        out_specs=pl.BlockSpec((tm, D), lambda i: (i, 0)),
        compiler_params=_cp(("parallel",), VMEM_BIG),
        name="ffn",
    )(h, shift, scale, gate, w_up, w_down)


PROJ_CHUNK = PW // 3


def _proj_body(h_ref, sh_ref, sc_ref, w_ref, o_ref):
    xm = (_rms(h_ref[...]) * (1.0 + sc_ref[...]) + sh_ref[...]).astype(bf16)
    for j in range(0, PW, PROJ_CHUNK):
        o_ref[:, j:j + PROJ_CHUNK] = _dot(xm, w_ref[:, j:j + PROJ_CHUNK]).astype(bf16)


def _proj(h, shift, scale, w, n_rows, n_batch):
    return pl.pallas_call(
        _proj_body,
        out_shape=jax.ShapeDtypeStruct((n_rows, PW), bf16),
        grid=(n_rows // TM,),
        in_specs=[pl.BlockSpec((TM, D), lambda i: (i, 0)), _mod_spec(n_batch), _mod_spec(n_batch),
                  _resident((D, PW))],
        out_specs=pl.BlockSpec((TM, PW), lambda i: (i, 0)),
        compiler_params=_cp(("parallel",), VMEM_BIG),
        name="proj",
    )(h, shift, scale, w)


def _group_mean_sq(x, gmat, width):
    x2 = x * x
    hi = x2.astype(bf16)
    lo = (x2 - hi.astype(f32)).astype(bf16)
    return (_dot(hi, gmat) + _dot(lo, gmat)) * (1.0 / width)


def _swap_halves(x, half):
    n = x.shape[-1]
    lane = lax.broadcasted_iota(jnp.int32, x.shape, 1)
    first = (lane % (2 * half)) < half
    return jnp.where(first, pltpu.roll(x, n - half, 1), pltpu.roll(x, half, 1))


def _rope(x, cos, sin_signed, half):
    return x * cos + _swap_halves(x, half) * sin_signed


def _softmax_pv(s, v):
    m = jnp.max(s, axis=-1, keepdims=True)
    e = jnp.exp(s - m)
    l = jnp.sum(e, axis=-1, keepdims=True)
    return _dot(e.astype(bf16), v) / l


def _q_row_block(n_batch):
    nq = S // TQ
    return lambda b, qi: jnp.where(qi < nq, b * nq + qi, n_batch * nq + b)


def _gqa_body(q_ref, kl_ref, kc_ref, vl_ref, vc_ref, qcos_ref, qsin_ref, kcos_ref, ksin_ref,
              qg_ref, kg_ref, gm_ref, o_ref, k_s, v_s):
    qi = pl.program_id(1)
    nq = S // TQ

    @pl.when(qi == 0)
    def _():
        gmk = gm_ref[:2 * HEAD, :2 * HEAD]
        kl = kl_ref[...].astype(f32)
        kn = kl * lax.rsqrt(_group_mean_sq(kl, gmk, HEAD) + RMS_EPS) * kg_ref[...]
        kn = _rope(kn, kcos_ref[...], ksin_ref[...], 16).astype(bf16)
        kc = kc_ref[...].astype(f32)
        kcn = (kc * lax.rsqrt(_group_mean_sq(kc, gmk, HEAD) + RMS_EPS) * kg_ref[...]).astype(bf16)
        for g in range(GQA_KVH):
            k_s[g, :S, :] = kn[:, g * HEAD:(g + 1) * HEAD]
            k_s[g, S:, :] = kcn[:, g * HEAD:(g + 1) * HEAD]
            v_s[g, :S, :] = vl_ref[:, g * HEAD:(g + 1) * HEAD]
            v_s[g, S:, :] = vc_ref[:, g * HEAD:(g + 1) * HEAD]

    q = q_ref[...].astype(f32)
    qn = q * lax.rsqrt(_group_mean_sq(q, gm_ref[...], HEAD) + RMS_EPS) * qg_ref[...]
    qr = (_rope(qn, qcos_ref[...], qsin_ref[...], 16) * (HEAD ** -0.5)).astype(bf16)

    def attend(lo):
        for h in range(GQA_H):
            g = h // GQA_GROUP
            s = _dot_t(qr[:, h * HEAD:(h + 1) * HEAD], k_s[g, lo:, :])
            o_ref[:, h * HEAD:(h + 1) * HEAD] = _softmax_pv(s, v_s[g, lo:, :]).astype(bf16)

    @pl.when(qi < nq)
    def _():
        attend(0)

    @pl.when(qi >= nq)
    def _():
        attend(S)


def _gqa(p, tabs, q_gain, k_gain, gmat, n_batch, with_ctx):
    nq = S // TQ
    n_rows = n_batch * S + (n_batch * NCTX if with_ctx else 0)
    qblk = _q_row_block(n_batch)
    ctx_blk = lambda b, qi: n_batch * nq + b
    qcos, qsin, kcos, ksin = tabs
    return pl.pallas_call(
        _gqa_body,
        out_shape=jax.ShapeDtypeStruct((n_rows, BW), bf16),
        grid=(n_batch, nq + (1 if with_ctx else 0)),
        in_specs=[
            pl.BlockSpec((TQ, BW), lambda b, qi: (qblk(b, qi), C_GQ // BW)),
            pl.BlockSpec((S, 128), lambda b, qi: (b, C_GK // 128)),
            pl.BlockSpec((NCTX, 128), lambda b, qi: (ctx_blk(b, qi), C_GK // 128)),
            pl.BlockSpec((S, 128), lambda b, qi: (b, C_GV // 128)),
            pl.BlockSpec((NCTX, 128), lambda b, qi: (ctx_blk(b, qi), C_GV // 128)),
            pl.BlockSpec((TQ, BW), lambda b, qi: (jnp.minimum(qi, nq), 0)),
            pl.BlockSpec((TQ, BW), lambda b, qi: (jnp.minimum(qi, nq), 0)),
            pl.BlockSpec((S, 128), lambda b, qi: (0, 0)),
            pl.BlockSpec((S, 128), lambda b, qi: (0, 0)),
            pl.BlockSpec((1, BW), lambda b, qi: (0, 0)),
            pl.BlockSpec((1, 128), lambda b, qi: (0, 0)),
            pl.BlockSpec((BW, BW), lambda b, qi: (0, 0)),
        ],
        out_specs=pl.BlockSpec((TQ, BW), lambda b, qi: (qblk(b, qi), 0)),
        scratch_shapes=[pltpu.VMEM((GQA_KVH, S + NCTX, HEAD), bf16),
                        pltpu.VMEM((GQA_KVH, S + NCTX, HEAD), bf16)],
        compiler_params=_cp(("parallel", "arbitrary"), VMEM_BIG),
        name="gqa",
    )(p, p, p, p, p, qcos, qsin, kcos, ksin, q_gain, k_gain, gmat)


def _mla_body(qa_ref, kval_ref, kvac_ref, krl_ref, krc_ref, qcos_ref, qsin_ref, kcos_ref, ksin_ref,
              qg_ref, kvg_ref, wq_ref, wkv_ref, o_ref, kn_s, v_s, kr_s):
    qi = pl.program_id(1)
    nq = S // TQ
    nw = MLA_H * MLA_NOPE

    @pl.when(qi == 0)
    def _():
        def kv(ref, lo, n):
            a = (_rms(ref[...].astype(f32)) * kvg_ref[...]).astype(bf16)
            kvh = _dot(a, wkv_ref[...]).astype(bf16)
            for h in range(MLA_H):
                kn_s[h, lo:lo + n, :] = kvh[:, h * MLA_NOPE:(h + 1) * MLA_NOPE]
                v_s[h, lo:lo + n, :] = kvh[:, nw + h * MLA_V:nw + (h + 1) * MLA_V]
        kv(kval_ref, 0, S)
        kv(kvac_ref, S, NCTX)
        kr = _rope(krl_ref[...].astype(f32), kcos_ref[...], ksin_ref[...], 8)
        kr_s[:S, :] = kr[:, :MLA_ROPE].astype(bf16)
        kr_s[S:, :] = krc_ref[:, :MLA_ROPE]

    qa = qa_ref[...].astype(f32)
    qan = qa * lax.rsqrt(jnp.sum(qa * qa, axis=-1, keepdims=True) * (1.0 / MLA_QR) + RMS_EPS)
    qh = _dot((qan * qg_ref[...]).astype(bf16), wq_ref[...])
    qn = (qh[:, :nw] * MLA_SCALE).astype(bf16)
    qr = (_rope(qh[:, nw:], qcos_ref[...], qsin_ref[...], 8) * MLA_SCALE).astype(bf16)

    def attend(lo):
        for h in range(MLA_H):
            s = (_dot_t(qn[:, h * MLA_NOPE:(h + 1) * MLA_NOPE], kn_s[h, lo:, :])
                 + _dot_t(qr[:, h * MLA_ROPE:(h + 1) * MLA_ROPE], kr_s[lo:, :]))
            o_ref[:, h * MLA_V:(h + 1) * MLA_V] = _softmax_pv(s, v_s[h, lo:, :]).astype(bf16)

    @pl.when(qi < nq)
    def _():
        attend(0)

    @pl.when(qi >= nq)
    def _():
        attend(S)


def _mla(p, tabs, q_gain, kv_gain, wq, wkv, n_batch, with_ctx):
    nq = S // TQ
    n_rows = n_batch * S + (n_batch * NCTX if with_ctx else 0)
    qblk = _q_row_block(n_batch)
    ctx_blk = lambda b, qi: n_batch * nq + b
    qcos, qsin, kcos, ksin = tabs
    rw = MLA_H * MLA_ROPE
    return pl.pallas_call(
        _mla_body,
        out_shape=jax.ShapeDtypeStruct((n_rows, BW), bf16),
        grid=(n_batch, nq + (1 if with_ctx else 0)),
        in_specs=[
            pl.BlockSpec((TQ, 512), lambda b, qi: (qblk(b, qi), C_MQA // 512)),
            pl.BlockSpec((S, MLA_KVR), lambda b, qi: (b, C_MKVA // MLA_KVR)),
            pl.BlockSpec((NCTX, MLA_KVR), lambda b, qi: (ctx_blk(b, qi), C_MKVA // MLA_KVR)),
            pl.BlockSpec((S, 128), lambda b, qi: (b, C_MKR // 128)),
            pl.BlockSpec((NCTX, 128), lambda b, qi: (ctx_blk(b, qi), C_MKR // 128)),
            pl.BlockSpec((TQ, rw), lambda b, qi: (jnp.minimum(qi, nq), 0)),
            pl.BlockSpec((TQ, rw), lambda b, qi: (jnp.minimum(qi, nq), 0)),
            pl.BlockSpec((S, 128), lambda b, qi: (0, 0)),
            pl.BlockSpec((S, 128), lambda b, qi: (0, 0)),
            pl.BlockSpec((1, 512), lambda b, qi: (0, 0)),
            pl.BlockSpec((1, MLA_KVR), lambda b, qi: (0, 0)),
            pl.BlockSpec((512, MLA_H * (MLA_NOPE + MLA_ROPE)), lambda b, qi: (0, 0)),
            pl.BlockSpec((MLA_KVR, MLA_H * (MLA_NOPE + MLA_V)), lambda b, qi: (0, 0)),
        ],
        out_specs=pl.BlockSpec((TQ, BW), lambda b, qi: (qblk(b, qi), 0)),
        scratch_shapes=[pltpu.VMEM((MLA_H, S + NCTX, MLA_NOPE), bf16),
                        pltpu.VMEM((MLA_H, S + NCTX, MLA_V), bf16),
                        pltpu.VMEM((S + NCTX, MLA_ROPE), bf16)],
        compiler_params=_cp(("parallel", "arbitrary"), VMEM_BIG),
        name="mla",
    )(p, p, p, p, p, qcos, qsin, kcos, ksin, q_gain, kv_gain, wq, wkv)


def _na_body(q_ref, k0_ref, k1_ref, k2_ref, kc_ref, v0_ref, v1_ref, v2_ref, vc_ref, bias_ref, o_ref):
    t = pl.program_id(0)
    nt = S // TQ
    q = q_ref[...] * (HEAD ** -0.5)

    @pl.when(t < nt)
    def _():
        kcat = jnp.concatenate([k0_ref[...], k1_ref[...], k2_ref[...], kc_ref[...]], axis=0)
        vcat = jnp.concatenate([v0_ref[...], v1_ref[...], v2_ref[...], vc_ref[...]], axis=0)
        for h in range(NA_H):
            hs = slice(h * HEAD, (h + 1) * HEAD)
            s = _dot_t(q[:, hs], kcat[:, hs]) + bias_ref[h]
            o_ref[:, hs] = _softmax_pv(s, vcat[:, hs]).astype(bf16)

    @pl.when(t >= nt)
    def _():
        for h in range(NA_H):
            hs = slice(h * HEAD, (h + 1) * HEAD)
            s = _dot_t(q[:, hs], kc_ref[:, hs])
            o_ref[:, hs] = _softmax_pv(s, vc_ref[:, hs]).astype(bf16)


def _na(p, bias, n_batch, with_ctx):
    nt = S // TQ
    n_rows = n_batch * S + (n_batch * NCTX if with_ctx else 0)
    qblk = lambda t, b: jnp.where(t < nt, b * nt + t, n_batch * nt + b)
    ctx_blk = lambda t, b: n_batch * nt + b
    centre = lambda t: jnp.clip(t, 1, nt - 2)
    variant = lambda t: jnp.where(t == 0, 0, jnp.where(t >= nt - 1, 2, 1))

    def win(col, d):
        return pl.BlockSpec((TQ, BW), lambda t, b: (b * nt + centre(t) + d, col // BW))

    return pl.pallas_call(
        _na_body,
        out_shape=jax.ShapeDtypeStruct((n_rows, BW), bf16),
        grid=(nt + (1 if with_ctx else 0), n_batch),
        in_specs=[
            pl.BlockSpec((TQ, BW), lambda t, b: (qblk(t, b), C_NQ // BW)),
            win(C_NK, -1), win(C_NK, 0), win(C_NK, 1),
            pl.BlockSpec((NCTX, BW), lambda t, b: (ctx_blk(t, b), C_NK // BW)),
            win(C_NV, -1), win(C_NV, 0), win(C_NV, 1),
            pl.BlockSpec((NCTX, BW), lambda t, b: (ctx_blk(t, b), C_NV // BW)),
            pl.BlockSpec((None, NA_H, TQ, 4 * TQ), lambda t, b: (variant(t), 0, 0, 0)),
        ],
        out_specs=pl.BlockSpec((TQ, BW), lambda t, b: (qblk(t, b), 0)),
        compiler_params=_cp(("arbitrary", "arbitrary"), VMEM_BIG),
        name="na",
    )(p, p, p, p, p, p, p, p, p, bias)


def _na_bias(rpb):
    rows_per_tile = TQ // GRID_W
    n_rows = S // GRID_W
    qi = np.arange(TQ)
    kj = np.arange(3 * TQ)
    qoff, qc = qi // GRID_W, qi % GRID_W
    koff, kc = kj // GRID_W, kj % GRID_W
    col_start = np.clip(qc - NA_COLS // 2, 0, GRID_W - NA_COLS)
    col_ok = (kc[None, :] >= col_start[:, None]) & (kc[None, :] < col_start[:, None] + NA_COLS)
    cidx = np.clip(kc[None, :] - qc[:, None] + NA_COLS - 1, 0, 2 * NA_COLS - 2)
    out = []
    for t in (0, 1, S // TQ - 1):
        centre = min(max(t, 1), S // TQ - 2)
        r = rows_per_tile * t + qoff
        start = np.clip(r - NA_ROWS // 2, 0, n_rows - NA_ROWS)
        krow = rows_per_tile * (centre - 1) + koff
        in_band = (krow[None, :] >= start[:, None]) & (krow[None, :] < start[:, None] + NA_ROWS)
        ridx = np.clip(krow[None, :] - r[:, None] + NA_ROWS - 1, 0, 2 * NA_ROWS - 2)
        b = rpb[:, ridx, cidx].astype(f32)
        b = jnp.where(jnp.asarray(in_band & col_ok), b, NEG)
        out.append(jnp.concatenate([b, jnp.zeros((NA_H, TQ, NCTX), f32)], axis=-1))
    return jnp.stack(out)


def _dft_tables(length):
    n = 2 * length
    f = jnp.arange(length, dtype=jnp.int32)[:, None]
    s = jnp.arange(length, dtype=jnp.int32)[None, :]
    ang = ((f * s) % n).astype(f32) * (2.0 * math.pi / n)
    a = jnp.cos(ang)
    bm = jnp.where(f == 0, jnp.where(s % 2 == 0, 1.0, -1.0), jnp.sin(ang))
    return a.astype(bf16), bm.astype(bf16), bm.T.astype(bf16)


def _filter_features(length):
    t = jnp.linspace(0.0, 1.0, length, dtype=f32)[:, None]
    w = (2.0 * math.pi / length) * jnp.arange(length, dtype=f32)[:, None]
    fb = jnp.linspace(1e-4, HY_BANDS - 1, HY_BANDS, dtype=f32)[None, :]
    z = jnp.concatenate([t, jnp.cos(fb * w), -jnp.sin(fb * w)], axis=-1)
    deltas = jnp.abs(jnp.linspace(HY_MIN_DECAY, HY_MAX_DECAY, HY_W, dtype=f32))
    dec = jnp.exp(-t * deltas)
    z = jnp.pad(z, ((0, 0), (0, 64 - HY_EMB)))
    z_rev = jnp.concatenate([z[:1], z[:0:-1]], axis=0)
    dec_rev = jnp.concatenate([jnp.zeros((1, HY_W), f32), dec[:0:-1]], axis=0)
    return z, z_rev, dec, dec_rev


def _split_bf16(x):
    hi = x.astype(bf16)
    return hi, (x - hi.astype(f32)).astype(bf16)


HY_CW = 128


def _hyfilt_body(z_ref, zr_ref, dec_ref, decr_ref, w1_ref, b_ref, fr_ref, wm_ref, wof_ref, wob_ref, a_ref, bm_ref,
                 kp_ref, kq_ref):
    length = z_ref.shape[0]

    def mlp(z, wo_ref):
        hdn = jnp.sin(fr_ref[0:1, :] * (_dot(z, w1_ref[...]) + b_ref[0:1, :]))
        hdn = jnp.sin(fr_ref[1:2, :] * (_dot(hdn, wm_ref[0]) + b_ref[1:2, :]))
        hdn = jnp.sin(fr_ref[2:3, :] * (_dot(hdn, wm_ref[1]) + b_ref[2:3, :]))
        return _dot(hdn, wo_ref[...])

    ff = mlp(z_ref[...], wof_ref) * dec_ref[...]
    fb = mlp(zr_ref[...], wob_ref) * decr_ref[...]
    norm = jnp.sum(jnp.abs(ff), axis=0, keepdims=True) + jnp.sum(jnp.abs(fb), axis=0, keepdims=True)
    ff = ff / norm
    fb = fb / norm
    row = lax.broadcasted_iota(jnp.int32, (length, HY_CW), 0)
    sign = jnp.where(row % 2 == 0, 1.0, -1.0)
    wgt = jnp.where(row == 0, 1.0, 2.0) * (1.0 / (2 * length))
    a = a_ref[...]
    bm = bm_ref[...]
    f_hi, f_lo = _split_bf16(ff)
    b_hi, b_lo = _split_bf16(fb)
    kp_ref[...] = ((_dot(a, f_hi) + _dot(a, f_lo)) + sign * (_dot(a, b_hi) + _dot(a, b_lo))) * wgt
    kq_ref[...] = ((_dot(bm, f_hi) + _dot(bm, f_lo)) + sign * (_dot(bm, b_hi) + _dot(bm, b_lo))) * wgt


def _hy_filter(length, feats, w1, fb, ffreq, wmid, wout, a, bm):
    z, z_rev, dec, dec_rev = feats
    nj = HY_W // HY_CW
    full = lambda shape: pl.BlockSpec(shape, lambda l, j: (0,) * len(shape))
    chan = lambda shape: pl.BlockSpec(shape, lambda l, j: (0, j))
    per_layer = lambda shape: pl.BlockSpec((None,) + shape, lambda l, j: (l,) + (0,) * len(shape))
    out_spec = pl.BlockSpec((None, length, HY_CW), lambda l, j: (l, 0, j))
    return pl.pallas_call(
        _hyfilt_body,
        out_shape=(jax.ShapeDtypeStruct((DEPTH, length, HY_W), f32),
                   jax.ShapeDtypeStruct((DEPTH, length, HY_W), f32)),
        grid=(DEPTH, nj),
        in_specs=[full((length, 64)), full((length, 64)), chan((length, HY_CW)), chan((length, HY_CW)),
                  per_layer((64, HY_HID)), per_layer((3, HY_HID)), per_layer((3, HY_HID)),
                  per_layer((2, HY_HID, HY_HID)),
                  pl.BlockSpec((None, HY_HID, HY_CW), lambda l, j: (l, 0, j)),
                  pl.BlockSpec((None, HY_HID, HY_CW), lambda l, j: (l, 0, nj + j)),
                  full((length, length)), full((length, length))],
        out_specs=(out_spec, out_spec),
        compiler_params=_cp(("parallel", "parallel"), VMEM_BIG),
        name="hyena_filter",
    )(z, z_rev, dec, dec_rev, w1, fb, ffreq, wmid, wout, wout, a, bm)


def _hyconv_body(x0_ref, x1_ref, v_ref, cw_ref, cb_ref, skip_ref, a_ref, bm_ref, bt_ref, kp_ref, kq_ref, o_ref):
    length = x0_ref.shape[0]
    row = lax.broadcasted_iota(jnp.int32, (length, HY_CW), 0)

    def short_conv(u_ref, part):
        u = u_ref[...].astype(f32)
        prev = jnp.where(row == 0, 0.0, pltpu.roll(u, 1, 0))
        nxt = jnp.where(row == length - 1, 0.0, pltpu.roll(u, length - 1, 0))
        w = cw_ref[part]
        return prev * w[0:1, :] + u * w[1:2, :] + nxt * w[2:3, :] + cb_ref[part]

    x0 = short_conv(x0_ref, 0)
    x1 = short_conv(x1_ref, 1)
    vv = short_conv(v_ref, 2) * x1
    vb = vv.astype(bf16)
    pc = _dot(a_ref[...], vb)
    qs = _dot(bm_ref[...], vb)
    kp = kp_ref[...]
    kq = kq_ref[...]
    first = row == 0
    yp = pc * kp - jnp.where(first, 0.0, qs * kq)
    yq = jnp.where(first, qs * kq, pc * kq + qs * kp)
    y = _dot(a_ref[...], yp.astype(bf16)) + _dot(bt_ref[...], yq.astype(bf16))
    o_ref[...] = ((y + vv * skip_ref[...]) * x0).astype(bf16)


def _hy_conv(p, length, row_blk0, n_seq, conv_w, conv_b, skip, tabs, kp, kq):
    a, bm, bt = tabs
    nj = HY_W // HY_CW
    c0 = C_HY // HY_CW

    def part(k):
        return pl.BlockSpec((length, HY_CW), lambda b, j: (row_blk0 + b, c0 + k * nj + j))

    chan3 = lambda shape: pl.BlockSpec(shape, lambda b, j: (0, 0, j))
    return pl.pallas_call(
        _hyconv_body,
        out_shape=jax.ShapeDtypeStruct((n_seq * length, HY_W), bf16),
        grid=(n_seq, nj),
        in_specs=[part(0), part(1), part(2),
                  chan3((3, 3, HY_CW)), chan3((3, 1, HY_CW)),
                  pl.BlockSpec((1, HY_CW), lambda b, j: (0, j)),
                  _resident((length, length)), _resident((length, length)), _resident((length, length)),
                  pl.BlockSpec((length, HY_CW), lambda b, j: (0, j)),
                  pl.BlockSpec((length, HY_CW), lambda b, j: (0, j))],
        out_specs=pl.BlockSpec((length, HY_CW), lambda b, j: (b, j)),
        compiler_params=_cp(("parallel", "arbitrary"), VMEM_BIG),
        name="hyena_conv",
    )(p, p, p, conv_w, conv_b, skip, a, bm, bt, kp, kq)


def _merge_body(og_ref, oh_ref, on_ref, om_ref, gate_ref, wb_ref, wo_ref, h_ref, m_ref, o_ref):
    merged = None
    for i, o in enumerate((og_ref, oh_ref, on_ref, om_ref)):
        g = jax.nn.sigmoid(gate_ref[:, i * D:(i + 1) * D].astype(f32))
        term = g * _dot(o[...], wb_ref[i])
        merged = term if merged is None else merged + term
    mix = _dot(merged.astype(bf16), wo_ref[...])
    o_ref[...] = h_ref[...] + m_ref[...] * mix


def _merge(outs, p, w_branch, w_out, h, mgate, n_rows, n_batch):
    ospec = pl.BlockSpec((TM, BW), lambda i: (i, 0))
    return pl.pallas_call(
        _merge_body,
        out_shape=jax.ShapeDtypeStruct((n_rows, D), f32),
        grid=(n_rows // TM,),
        in_specs=[ospec, ospec, ospec, ospec,
                  pl.BlockSpec((TM, 4 * D), lambda i: (i, C_GATE // (4 * D))),
                  _resident((4, BW, D)), _resident((D, D)),
                  pl.BlockSpec((TM, D), lambda i: (i, 0)), _mod_spec(n_batch)],
        out_specs=pl.BlockSpec((TM, D), lambda i: (i, 0)),
        compiler_params=_cp(("parallel",), VMEM_BIG),
        name="merge",
    )(*outs, p, w_branch, w_out, h, mgate)


def _final_body(h_ref, g_ref, o_ref):
    o_ref[...] = _rms(h_ref[...]) * g_ref[...]


def _final_norm(h, gain, n_rows):
    return pl.pallas_call(
        _final_body,
        out_shape=jax.ShapeDtypeStruct((n_rows, D), f32),
        grid=(n_rows // TM,),
        in_specs=[pl.BlockSpec((TM, D), lambda i: (i, 0)), pl.BlockSpec((1, D), lambda i: (0, 0))],
        out_specs=pl.BlockSpec((TM, D), lambda i: (i, 0)),
        compiler_params=_cp(("parallel",)),
        name="final_norm",
    )(h, gain)


def _rope_table(half, n_heads):
    freqs = ROPE_THETA ** (-jnp.arange(half, dtype=f32) / half)
    pos = jnp.arange(S)
    ar = (pos // GRID_W).astype(f32)[:, None] * freqs
    ac = (pos % GRID_W).astype(f32)[:, None] * freqs
    cos_h = jnp.concatenate([jnp.cos(ar), jnp.cos(ar), jnp.cos(ac), jnp.cos(ac)], axis=-1)
    sin_h = jnp.concatenate([-jnp.sin(ar), jnp.sin(ar), -jnp.sin(ac), jnp.sin(ac)], axis=-1)
    return jnp.tile(cos_h, (1, n_heads)), jnp.tile(sin_h, (1, n_heads))


def _with_identity_rows(cos, sin, n):
    return (jnp.concatenate([cos, jnp.ones((n, cos.shape[1]), f32)], axis=0),
            jnp.concatenate([sin, jnp.zeros((n, sin.shape[1]), f32)], axis=0))


def _pad_lanes(cos, sin, width):
    extra = width - cos.shape[1]
    return (jnp.concatenate([cos, jnp.ones((S, extra), f32)], axis=1),
            jnp.concatenate([sin, jnp.zeros((S, extra), f32)], axis=1))


def _pack_w_in(w_in):
    z = lambda n: jnp.zeros(w_in.shape[:-1] + (n,), w_in.dtype)
    sl = lambda a, b: w_in[..., a:b]
    return jnp.concatenate([
        sl(4512, 8608),
        sl(768, 2304),
        sl(0, 512),
        sl(2304, 3840),
        sl(4224, 4480),
        sl(512, 768),
        sl(3840, 4224), z(128),
        sl(4480, 4512), z(96),
    ], axis=-1).astype(bf16)


def _pack_wq(wq_b):
    w = wq_b.reshape(DEPTH, MLA_QR, MLA_H, MLA_NOPE + MLA_ROPE)
    w = jnp.concatenate([w[..., :MLA_NOPE].reshape(DEPTH, MLA_QR, -1),
                         w[..., MLA_NOPE:].reshape(DEPTH, MLA_QR, -1)], axis=-1)
    return jnp.pad(w, ((0, 0), (0, 512 - MLA_QR), (0, 0))).astype(bf16)


def _pack_wkv(wkv_b):
    w = wkv_b.reshape(DEPTH, MLA_KVR, MLA_H, MLA_NOPE + MLA_V)
    return jnp.concatenate([w[..., :MLA_NOPE].reshape(DEPTH, MLA_KVR, -1),
                            w[..., MLA_NOPE:].reshape(DEPTH, MLA_KVR, -1)], axis=-1).astype(bf16)


def kernel(x, c, ctx, c_ctx, ada_w, ada_b, ffn1_up, ffn1_down, w_in, gqa_q_norm, gqa_k_norm, hy_conv_w, hy_conv_b, hy_f_w1, hy_f_b, hy_f_freq, hy_f_w_mid, hy_f_w_out, hy_skip, na_rpb, mla_q_norm, mla_kv_norm, mla_wq_b, mla_wkv_b, w_branch, w_out, ffn2_up, ffn2_down, final_norm):
    nb = x.shape[0]
    rows_lat = nb * S
    rows_all = rows_lat + nb * NCTX

    cond = jnp.concatenate([c, c_ctx[None], jnp.zeros((16 - nb - 1, D), f32)], axis=0)
    mod = _adaln(cond, ada_w, ada_b)[:, :nb + 1].reshape(DEPTH, nb + 1, N_MOD, 1, D)
    mods = lambda l, k: mod[l, :, k]

    w1u, w1d = ffn1_up.astype(bf16), ffn1_down.astype(bf16)
    w2u, w2d = ffn2_up.astype(bf16), ffn2_down.astype(bf16)
    w_in_p = _pack_w_in(w_in)
    wq_p, wkv_p = _pack_wq(mla_wq_b), _pack_wkv(mla_wkv_b)
    wb, wo = w_branch.astype(bf16), w_out.astype(bf16)

    gq_tab = _with_identity_rows(*_rope_table(16, GQA_H), NCTX)
    gk_tab = _rope_table(16, GQA_KVH)
    mq_tab = _with_identity_rows(*_rope_table(8, MLA_H), NCTX)
    mk_tab = _pad_lanes(*_rope_table(8, 1), 128)
    lane = jnp.arange(BW)
    gmat = (lane[:, None] // HEAD == lane[None, :] // HEAD).astype(bf16)

    hy_w1 = jnp.pad(hy_f_w1, ((0, 0), (0, 64 - HY_EMB), (0, 0)))
    dft_lat, dft_ctx = _dft_tables(S), _dft_tables(NCTX)
    kp_lat, kq_lat = _hy_filter(S, _filter_features(S), hy_w1, hy_f_b, hy_f_freq, hy_f_w_mid, hy_f_w_out,
                                dft_lat[0], dft_lat[1])
    kp_ctx, kq_ctx = _hy_filter(NCTX, _filter_features(NCTX), hy_w1, hy_f_b, hy_f_freq, hy_f_w_mid, hy_f_w_out,
                                dft_ctx[0], dft_ctx[1])
    conv_w = hy_conv_w.reshape(DEPTH, 3, 3, HY_W).transpose(0, 2, 1, 3)
    conv_b = hy_conv_b.reshape(DEPTH, 3, 1, HY_W)

    h = jnp.concatenate([x.reshape(rows_lat, D), ctx.reshape(nb * NCTX, D)], axis=0)
    for l in range(DEPTH):
        with_ctx = l < DEPTH - 1
        n_out = rows_all if with_ctx else rows_lat
        h = _ffn(h, mods(l, 0), mods(l, 1), mods(l, 2), w1u[l], w1d[l], rows_all, nb)
        p = _proj(h, mods(l, 3), mods(l, 4), w_in_p[l], rows_all, nb)
        o_gqa = _gqa(p, gq_tab + gk_tab, jnp.tile(gqa_q_norm[l], GQA_H)[None], jnp.tile(gqa_k_norm[l], GQA_KVH)[None],
                     gmat, nb, with_ctx)
        o_hy = _hy_conv(p, S, 0, nb, conv_w[l], conv_b[l], hy_skip[l][None], dft_lat, kp_lat[l], kq_lat[l])
        if with_ctx:
            o_hy_c = _hy_conv(p, NCTX, rows_lat // NCTX, nb, conv_w[l], conv_b[l], hy_skip[l][None], dft_ctx,
                              kp_ctx[l], kq_ctx[l])
            o_hy = jnp.concatenate([o_hy, o_hy_c], axis=0)
        o_na = _na(p, _na_bias(na_rpb[l]), nb, with_ctx)
        o_mla = _mla(p, mq_tab + mk_tab, jnp.pad(mla_q_norm[l], (0, 512 - MLA_QR))[None], mla_kv_norm[l][None],
                     wq_p[l], wkv_p[l], nb, with_ctx)
        h = _merge((o_gqa, o_hy, o_na, o_mla), p, wb[l], wo[l], h, mods(l, 5), n_out, nb)
        h = _ffn(h, mods(l, 6), mods(l, 7), mods(l, 8), w2u[l], w2d[l], n_out, nb)
    out = _final_norm(h, final_norm[None], rows_lat)
    return out.reshape(nb, S, D)
```

```python
import functools
import math

import numpy as np
import jax
import jax.numpy as jnp
from jax import lax
from jax.experimental import pallas as pl
from jax.experimental.pallas import tpu as pltpu

f32 = jnp.float32
bf16 = jnp.bfloat16

D = 1024
S = 2048
DEPTH = 4
GRID_W = 64
NCTX = 256
HEAD = 64
ROPE_THETA = 10000.0
RMS_EPS = 1e-6
N_MOD = 9
F = 2816
BW = 512
GQA_H, GQA_KVH, GQA_GROUP = 8, 2, 4
NA_H, NA_ROWS, NA_COLS = 8, 8, 16
MLA_H, MLA_NOPE, MLA_ROPE, MLA_V = 8, 64, 32, 64
MLA_QR, MLA_KVR = 384, 256
MLA_SCALE = (MLA_NOPE + MLA_ROPE) ** -0.5
MLA_SLOT = 128
LOG2E = math.log2(math.e)
HY_W = 512
HY_BANDS = 16
HY_EMB = 1 + 2 * HY_BANDS
HY_HID = 64
HY_MIN_DECAY = math.log(1e-2) / 1.5
HY_MAX_DECAY = math.log(1e-2) / 0.3
NEG = -1e30

C_GATE = 0
C_HY = 4096
C_GQ = 5632
C_NQ, C_NK, C_NV = 6144, 6656, 7168
C_MKVA = 7680
C_GK, C_GV = 7936, 8064
C_MQA = 8192
C_MKR = 8704
PW = 8832

TQ = 256
TM = 512
VMEM_BIG = 56 * 1024 * 1024


def _cp(sem, vmem=None):
    return pltpu.CompilerParams(dimension_semantics=sem, vmem_limit_bytes=vmem)


def _resident(shape):
    nd = len(shape)
    return pl.BlockSpec(shape, lambda *_: (0,) * nd, pipeline_mode=pl.Buffered(1))


def _dot(a, b):
    return jnp.dot(a, b, preferred_element_type=f32)


def _dot_t(a, b):
    return lax.dot_general(a, b, (((1,), (1,)), ((), ())), preferred_element_type=f32)


def _rms(x):
    return x * lax.rsqrt(jnp.mean(x * x, axis=-1, keepdims=True) + RMS_EPS)


def _group_of_tile(i, n_batch):
    return jnp.minimum((i * TM) // S, n_batch)


def _mod_spec(n_batch):
    return pl.BlockSpec((None, 1, D), lambda i: (_group_of_tile(i, n_batch), 0, 0))


def _ada_body(c_ref, w_ref, b_ref, o_ref):
    x = c_ref[...]
    xs = (x * jax.nn.sigmoid(x)).astype(bf16)
    o_ref[...] = _dot(xs, w_ref[...].astype(bf16)) + b_ref[...]


def _adaln(cond, ada_w, ada_b):
    tn = 1024
    return pl.pallas_call(
        _ada_body,
        out_shape=jax.ShapeDtypeStruct((DEPTH, 16, N_MOD * D), f32),
        grid=(DEPTH, N_MOD * D // tn),
        in_specs=[pl.BlockSpec((16, D), lambda l, j: (0, 0)),
                  pl.BlockSpec((None, D, tn), lambda l, j: (l, 0, j)),
                  pl.BlockSpec((None, 1, tn), lambda l, j: (l, 0, j))],
        out_specs=pl.BlockSpec((None, 16, tn), lambda l, j: (l, 0, j)),
        compiler_params=_cp(("parallel", "parallel")),
        name="adaln",
    )(cond, ada_w, ada_b.reshape(DEPTH, 1, N_MOD * D))


def _ffn_body(h_ref, sh_ref, sc_ref, gt_ref, wup_ref, wdn_ref, o_ref):
    h = h_ref[...]
    xm = (_rms(h) * (1.0 + sc_ref[...]) + sh_ref[...]).astype(bf16)
    a = _dot(xm, wup_ref[:, :F])
    g = _dot(xm, wup_ref[:, F:])
    mid = (a * jax.nn.sigmoid(a) * g).astype(bf16)
    y = _dot(mid, wdn_ref[...])
    o_ref[...] = h + (0.5 * gt_ref[...]) * y


def _ffn(h, shift, scale, gate, w_up, w_down, n_rows, n_batch):
    tm = 256
    grp = lambda i: (jnp.minimum((i * tm) // S, n_batch), 0, 0)
    mspec = pl.BlockSpec((None, 1, D), grp)
    return pl.pallas_call(
        _ffn_body,
        out_shape=jax.ShapeDtypeStruct((n_rows, D), f32),
        grid=(n_rows // tm,),
        in_specs=[pl.BlockSpec((tm, D), lambda i: (i, 0)), mspec, mspec, mspec,
                  _resident((D, 2 * F)), _resident((F, D))],
        out_specs=pl.BlockSpec((tm, D), lambda i: (i, 0)),
        compiler_params=_cp(("parallel",), VMEM_BIG),
        name="ffn",
    )(h, shift, scale, gate, w_up, w_down)


PROJ_CHUNK = PW // 3


def _proj_body(h_ref, sh_ref, sc_ref, w_ref, o_ref):
    xm = (_rms(h_ref[...]) * (1.0 + sc_ref[...]) + sh_ref[...]).astype(bf16)
    for j in range(0, PW, PROJ_CHUNK):
        o_ref[:, j:j + PROJ_CHUNK] = _dot(xm, w_ref[:, j:j + PROJ_CHUNK]).astype(bf16)


def _proj(h, shift, scale, w, n_rows, n_batch):
    return pl.pallas_call(
        _proj_body,
        out_shape=jax.ShapeDtypeStruct((n_rows, PW), bf16),
        grid=(n_rows // TM,),
        in_specs=[pl.BlockSpec((TM, D), lambda i: (i, 0)), _mod_spec(n_batch), _mod_spec(n_batch),
                  _resident((D, PW))],
        out_specs=pl.BlockSpec((TM, PW), lambda i: (i, 0)),
        compiler_params=_cp(("parallel",), VMEM_BIG),
        name="proj",
    )(h, shift, scale, w)


def _group_mean_sq(x, gmat, width):
    x2 = x * x
    hi = x2.astype(bf16)
    lo = (x2 - hi.astype(f32)).astype(bf16)
    return (_dot(hi, gmat) + _dot(lo, gmat)) * (1.0 / width)


def _swap_halves(x, half):
    n = x.shape[-1]
    lane = lax.broadcasted_iota(jnp.int32, x.shape, 1)
    first = (lane % (2 * half)) < half
    return jnp.where(first, pltpu.roll(x, n - half, 1), pltpu.roll(x, half, 1))


def _rope(x, cos, sin_signed, half):
    return x * cos + _swap_halves(x, half) * sin_signed


def _softmax_pv(s, v):
    m = jnp.max(s, axis=-1, keepdims=True)
    e = jnp.exp2(s - m)
    l = jnp.sum(e, axis=-1, keepdims=True)
    return _dot(e.astype(bf16), v) / l


def _softmax_pv_ones(s, v_ones, dv):
    m = jnp.max(s, axis=-1, keepdims=True)
    e = jnp.exp2(s - m).astype(bf16)
    oa = _dot(e, v_ones)
    return oa[:, :dv] / oa[:, dv:dv + 1]


def _q_row_block(n_batch):
    nq = S // TQ
    return lambda b, qi: jnp.where(qi < nq, b * nq + qi, n_batch * nq + b)


def _gqa_body(q_ref, kl_ref, kc_ref, vl_ref, vc_ref, qcos_ref, qsin_ref, kcos_ref, ksin_ref,
              qg_ref, kg_ref, gm_ref, o_ref, k_s, v_s):
    qi = pl.program_id(1)
    nq = S // TQ

    @pl.when(qi == 0)
    def _():
        gmk = gm_ref[:2 * HEAD, :2 * HEAD]
        kl = kl_ref[...].astype(f32)
        kn = kl * lax.rsqrt(_group_mean_sq(kl, gmk, HEAD) + RMS_EPS) * kg_ref[...]
        kn = _rope(kn, kcos_ref[...], ksin_ref[...], 16).astype(bf16)
        kc = kc_ref[...].astype(f32)
        kcn = (kc * lax.rsqrt(_group_mean_sq(kc, gmk, HEAD) + RMS_EPS) * kg_ref[...]).astype(bf16)
        for g in range(GQA_KVH):
            k_s[g, :S, :] = kn[:, g * HEAD:(g + 1) * HEAD]
            k_s[g, S:, :] = kcn[:, g * HEAD:(g + 1) * HEAD]
            v_s[g, :S, :HEAD] = vl_ref[:, g * HEAD:(g + 1) * HEAD]
            v_s[g, S:, :HEAD] = vc_ref[:, g * HEAD:(g + 1) * HEAD]
            v_s[g, :, HEAD:] = jnp.ones((S + NCTX, HEAD), bf16)

    q = q_ref[...].astype(f32)
    qn = q * lax.rsqrt(_group_mean_sq(q, gm_ref[...], HEAD) + RMS_EPS) * qg_ref[...]
    qr = (_rope(qn, qcos_ref[...], qsin_ref[...], 16) * (HEAD ** -0.5 * LOG2E)).astype(bf16)

    def attend(lo):
        for h in range(GQA_H):
            g = h // GQA_GROUP
            s = _dot_t(qr[:, h * HEAD:(h + 1) * HEAD], k_s[g, lo:, :])
            o_ref[:, h * HEAD:(h + 1) * HEAD] = _softmax_pv_ones(s, v_s[g, lo:, :], HEAD).astype(bf16)

    @pl.when(qi < nq)
    def _():
        attend(0)

    @pl.when(qi >= nq)
    def _():
        attend(S)


def _gqa(p, tabs, q_gain, k_gain, gmat, n_batch, with_ctx):
    nq = S // TQ
    n_rows = n_batch * S + (n_batch * NCTX if with_ctx else 0)
    qblk = _q_row_block(n_batch)
    ctx_blk = lambda b, qi: n_batch * nq + b
    qcos, qsin, kcos, ksin = tabs
    return pl.pallas_call(
        _gqa_body,
        out_shape=jax.ShapeDtypeStruct((n_rows, BW), bf16),
        grid=(n_batch, nq + (1 if with_ctx else 0)),
        in_specs=[
            pl.BlockSpec((TQ, BW), lambda b, qi: (qblk(b, qi), C_GQ // BW)),
            pl.BlockSpec((S, 128), lambda b, qi: (b, C_GK // 128)),
            pl.BlockSpec((NCTX, 128), lambda b, qi: (ctx_blk(b, qi), C_GK // 128)),
            pl.BlockSpec((S, 128), lambda b, qi: (b, C_GV // 128)),
            pl.BlockSpec((NCTX, 128), lambda b, qi: (ctx_blk(b, qi), C_GV // 128)),
            pl.BlockSpec((TQ, BW), lambda b, qi: (jnp.minimum(qi, nq), 0)),
            pl.BlockSpec((TQ, BW), lambda b, qi: (jnp.minimum(qi, nq), 0)),
            pl.BlockSpec((S, 128), lambda b, qi: (0, 0)),
            pl.BlockSpec((S, 128), lambda b, qi: (0, 0)),
            pl.BlockSpec((1, BW), lambda b, qi: (0, 0)),
            pl.BlockSpec((1, 128), lambda b, qi: (0, 0)),
            pl.BlockSpec((BW, BW), lambda b, qi: (0, 0)),
        ],
        out_specs=pl.BlockSpec((TQ, BW), lambda b, qi: (qblk(b, qi), 0)),
        scratch_shapes=[pltpu.VMEM((GQA_KVH, S + NCTX, HEAD), bf16),
                        pltpu.VMEM((GQA_KVH, S + NCTX, 2 * HEAD), bf16)],
        compiler_params=_cp(("parallel", "arbitrary"), VMEM_BIG),
        name="gqa",
    )(p, p, p, p, p, qcos, qsin, kcos, ksin, q_gain, k_gain, gmat)


def _mla_body(qa_ref, kval_ref, kvac_ref, krl_ref, krc_ref, qcos_ref, qsin_ref, kcos_ref, ksin_ref,
              qg_ref, kvg_ref, wq_ref, wkv_ref, o_ref, k_s, v_s):
    qi = pl.program_id(1)
    nq = S // TQ
    slot = MLA_SLOT

    @pl.when(qi == 0)
    def _():
        lane = lax.broadcasted_iota(jnp.int32, (1, slot), 1)
        ones_hi = jnp.where(lane >= MLA_V, 1.0, 0.0)

        def kv(ref, kr, lo, n):
            a = (_rms(ref[...].astype(f32)) * kvg_ref[...]).astype(bf16)
            for hp in range(MLA_H // 2):
                kk = _dot(a, wkv_ref[:, hp * 2 * slot:(hp + 1) * 2 * slot])
                vv = _dot(a, wkv_ref[:, (MLA_H + hp * 2) * slot:(MLA_H + hp * 2 + 2) * slot])
                for i in range(2):
                    k_s[2 * hp + i, lo:lo + n, :] = (kk[:, i * slot:(i + 1) * slot] + kr).astype(bf16)
                    v_s[2 * hp + i, lo:lo + n, :] = (vv[:, i * slot:(i + 1) * slot] + ones_hi).astype(bf16)

        kv(kval_ref, _rope(krl_ref[...].astype(f32), kcos_ref[...], ksin_ref[...], 8), 0, S)
        kv(kvac_ref, krc_ref[...].astype(f32), S, NCTX)

    qa = qa_ref[...].astype(f32)
    qan = qa * lax.rsqrt(jnp.sum(qa * qa, axis=-1, keepdims=True) * (1.0 / MLA_QR) + RMS_EPS)
    qh = _dot((qan * qg_ref[...]).astype(bf16), wq_ref[...])
    q = (_rope(qh, qcos_ref[...], qsin_ref[...], 8) * (MLA_SCALE * LOG2E)).astype(bf16)

    def attend(lo):
        for h in range(MLA_H):
            s = _dot_t(q[:, h * slot:(h + 1) * slot], k_s[h, lo:, :])
            o_ref[:, h * MLA_V:(h + 1) * MLA_V] = _softmax_pv_ones(s, v_s[h, lo:, :], MLA_V).astype(bf16)

    @pl.when(qi < nq)
    def _():
        attend(0)

    @pl.when(qi >= nq)
    def _():
        attend(S)


def _mla(p, tabs, q_gain, kv_gain, wq, wkv, n_batch, with_ctx):
    nq = S // TQ
    n_rows = n_batch * S + (n_batch * NCTX if with_ctx else 0)
    qblk = _q_row_block(n_batch)
    ctx_blk = lambda b, qi: n_batch * nq + b
    qcos, qsin, kcos, ksin = tabs
    rw = MLA_H * MLA_SLOT
    return pl.pallas_call(
        _mla_body,
        out_shape=jax.ShapeDtypeStruct((n_rows, BW), bf16),
        grid=(n_batch, nq + (1 if with_ctx else 0)),
        in_specs=[
            pl.BlockSpec((TQ, 512), lambda b, qi: (qblk(b, qi), C_MQA // 512)),
            pl.BlockSpec((S, MLA_KVR), lambda b, qi: (b, C_MKVA // MLA_KVR)),
            pl.BlockSpec((NCTX, MLA_KVR), lambda b, qi: (ctx_blk(b, qi), C_MKVA // MLA_KVR)),
            pl.BlockSpec((S, 128), lambda b, qi: (b, C_MKR // 128)),
            pl.BlockSpec((NCTX, 128), lambda b, qi: (ctx_blk(b, qi), C_MKR // 128)),
            pl.BlockSpec((TQ, rw), lambda b, qi: (jnp.minimum(qi, nq), 0)),
            pl.BlockSpec((TQ, rw), lambda b, qi: (jnp.minimum(qi, nq), 0)),
            pl.BlockSpec((S, 128), lambda b, qi: (0, 0)),
            pl.BlockSpec((S, 128), lambda b, qi: (0, 0)),
            pl.BlockSpec((1, 512), lambda b, qi: (0, 0)),
            pl.BlockSpec((1, MLA_KVR), lambda b, qi: (0, 0)),
            pl.BlockSpec((512, MLA_H * MLA_SLOT), lambda b, qi: (0, 0)),
            pl.BlockSpec((MLA_KVR, 2 * MLA_H * MLA_SLOT), lambda b, qi: (0, 0)),
        ],
        out_specs=pl.BlockSpec((TQ, BW), lambda b, qi: (qblk(b, qi), 0)),
        scratch_shapes=[pltpu.VMEM((MLA_H, S + NCTX, MLA_SLOT), bf16),
                        pltpu.VMEM((MLA_H, S + NCTX, MLA_SLOT), bf16)],
        compiler_params=_cp(("parallel", "arbitrary"), VMEM_BIG),
        name="mla",
    )(p, p, p, p, p, qcos, qsin, kcos, ksin, q_gain, kv_gain, wq, wkv)


def _na_body(q_ref, k0_ref, k1_ref, k2_ref, kc_ref, v0_ref, v1_ref, v2_ref, vc_ref, bias_ref, o_ref):
    t = pl.program_id(0)
    nt = S // TQ
    q = q_ref[...]

    @pl.when(t < nt)
    def _():
        kcat = jnp.concatenate([k0_ref[...], k1_ref[...], k2_ref[...], kc_ref[...]], axis=0)
        vcat = jnp.concatenate([v0_ref[...], v1_ref[...], v2_ref[...], vc_ref[...]], axis=0)
        for h in range(NA_H):
            hs = slice(h * HEAD, (h + 1) * HEAD)
            s = _dot_t(q[:, hs], kcat[:, hs]) + bias_ref[h]
            o_ref[:, hs] = _softmax_pv(s, vcat[:, hs]).astype(bf16)

    @pl.when(t >= nt)
    def _():
        for h in range(NA_H):
            hs = slice(h * HEAD, (h + 1) * HEAD)
            s = _dot_t(q[:, hs], kc_ref[:, hs])
            o_ref[:, hs] = _softmax_pv(s, vc_ref[:, hs]).astype(bf16)


def _na(p, bias, n_batch, with_ctx):
    nt = S // TQ
    n_rows = n_batch * S + (n_batch * NCTX if with_ctx else 0)
    qblk = lambda t, b: jnp.where(t < nt, b * nt + t, n_batch * nt + b)
    ctx_blk = lambda t, b: n_batch * nt + b
    centre = lambda t: jnp.clip(t, 1, nt - 2)
    variant = lambda t: jnp.where(t == 0, 0, jnp.where(t >= nt - 1, 2, 1))

    def win(col, d):
        return pl.BlockSpec((TQ, BW), lambda t, b: (b * nt + centre(t) + d, col // BW))

    return pl.pallas_call(
        _na_body,
        out_shape=jax.ShapeDtypeStruct((n_rows, BW), bf16),
        grid=(nt + (1 if with_ctx else 0), n_batch),
        in_specs=[
            pl.BlockSpec((TQ, BW), lambda t, b: (qblk(t, b), C_NQ // BW)),
            win(C_NK, -1), win(C_NK, 0), win(C_NK, 1),
            pl.BlockSpec((NCTX, BW), lambda t, b: (ctx_blk(t, b), C_NK // BW)),
            win(C_NV, -1), win(C_NV, 0), win(C_NV, 1),
            pl.BlockSpec((NCTX, BW), lambda t, b: (ctx_blk(t, b), C_NV // BW)),
            pl.BlockSpec((None, NA_H, TQ, 4 * TQ), lambda t, b: (variant(t), 0, 0, 0)),
        ],
        out_specs=pl.BlockSpec((TQ, BW), lambda t, b: (qblk(t, b), 0)),
        compiler_params=_cp(("arbitrary", "arbitrary"), VMEM_BIG),
        name="na",
    )(p, p, p, p, p, p, p, p, p, bias)


def _na_bias(rpb):
    rows_per_tile = TQ // GRID_W
    n_rows = S // GRID_W
    qc = np.arange(GRID_W)[:, None]
    kc = np.arange(GRID_W)[None, :]
    col_start = np.clip(qc - NA_COLS // 2, 0, GRID_W - NA_COLS)
    col_ok = (kc >= col_start) & (kc < col_start + NA_COLS)
    cidx = np.clip(kc - qc + NA_COLS - 1, 0, 2 * NA_COLS - 2)
    onehot = (np.arange(2 * NA_COLS - 1)[:, None, None] == cidx[None]).astype(np.float32)
    blocks = jnp.einsum('hrc,cqk->hrqk', rpb.astype(f32), jnp.asarray(onehot), precision=lax.Precision.HIGHEST)
    blocks = jnp.where(jnp.asarray(col_ok), blocks * LOG2E, NEG)
    masked = jnp.full((NA_H, GRID_W, GRID_W), NEG, f32)
    out = []
    for t in (0, 1, S // TQ - 1):
        centre = min(max(t, 1), S // TQ - 2)
        tile_rows = []
        for qoff in range(rows_per_tile):
            r = rows_per_tile * t + qoff
            start = min(max(r - NA_ROWS // 2, 0), n_rows - NA_ROWS)
            row = []
            for koff in range(3 * rows_per_tile):
                krow = rows_per_tile * (centre - 1) + koff
                in_band = start <= krow < start + NA_ROWS
                row.append(blocks[:, krow - r + NA_ROWS - 1] if in_band else masked)
            row.append(jnp.zeros((NA_H, GRID_W, NCTX), f32))
            tile_rows.append(jnp.concatenate(row, axis=-1))
        out.append(jnp.concatenate(tile_rows, axis=-2))
    return jnp.stack(out)


def _dft_tables(length):
    n = 2 * length
    f = jnp.arange(length, dtype=jnp.int32)[:, None]
    s = jnp.arange(length, dtype=jnp.int32)[None, :]
    ang = ((f * s) % n).astype(f32) * (2.0 * math.pi / n)
    return jnp.cos(ang).astype(bf16), jnp.sin(ang).astype(bf16)


def _filter_features(length):
    t = jnp.linspace(0.0, 1.0, length, dtype=f32)[:, None]
    w = (2.0 * math.pi / length) * jnp.arange(length, dtype=f32)[:, None]
    fb = jnp.linspace(1e-4, HY_BANDS - 1, HY_BANDS, dtype=f32)[None, :]
    z = jnp.concatenate([t, jnp.cos(fb * w), -jnp.sin(fb * w)], axis=-1)
    deltas = jnp.abs(jnp.linspace(HY_MIN_DECAY, HY_MAX_DECAY, HY_W, dtype=f32))
    dec = jnp.exp(-t * deltas)
    z = jnp.pad(z, ((0, 0), (0, 64 - HY_EMB)))
    z_rev = jnp.concatenate([z[:1], z[:0:-1]], axis=0)
    dec_rev = jnp.concatenate([jnp.zeros((1, HY_W), f32), dec[:0:-1]], axis=0)
    return z, z_rev, dec, dec_rev


def _split_bf16(x):
    hi = x.astype(bf16)
    return hi, (x - hi.astype(f32)).astype(bf16)


HY_CW = 256
HY_FC = 512


def _alternating(shape):
    row = lax.broadcasted_iota(jnp.int32, shape, 0)
    return jnp.where(row % 2 == 0, 1.0, -1.0)


def _hyfilt_body(z_ref, zr_ref, dec_ref, decr_ref, w1_ref, b_ref, fr_ref, wm_ref, wof_ref, wob_ref, a_ref, sn_ref,
                 kp_ref, kq_ref, hf_s, hb_s):
    length = z_ref.shape[0]
    fc = min(HY_FC, length)

    def hidden(z):
        hdn = jnp.sin(fr_ref[0:1, :] * (_dot(z, w1_ref[...]) + b_ref[0:1, :]))
        hdn = jnp.sin(fr_ref[1:2, :] * (_dot(hdn, wm_ref[0]) + b_ref[1:2, :]))
        return jnp.sin(fr_ref[2:3, :] * (_dot(hdn, wm_ref[1]) + b_ref[2:3, :]))

    @pl.when(pl.program_id(1) == 0)
    def _():
        hf_s[...] = hidden(z_ref[...])
        hb_s[...] = hidden(zr_ref[...])

    ff = _dot(hf_s[...], wof_ref[...]) * dec_ref[...]
    fb = _dot(hb_s[...], wob_ref[...]) * decr_ref[...]
    norm = jnp.sum(jnp.abs(ff), axis=0, keepdims=True) + jnp.sum(jnp.abs(fb), axis=0, keepdims=True)
    ff = ff / norm
    fb = fb / norm
    inv_n = 1.0 / (2 * length)
    nyq = jnp.sum(_alternating((length, HY_CW)) * (ff + fb), axis=0, keepdims=True) * inv_n
    f_hi, f_lo = _split_bf16(ff)
    b_hi, b_lo = _split_bf16(fb)
    sign = _alternating((fc, HY_CW))
    for c in range(length // fc):
        rows = slice(c * fc, (c + 1) * fc)
        a = a_ref[rows, :]
        sn = sn_ref[rows, :]
        row = lax.broadcasted_iota(jnp.int32, (fc, HY_CW), 0) + c * fc
        wgt = jnp.where(row == 0, inv_n, 2.0 * inv_n)
        kp_ref[rows, :] = ((_dot(a, f_hi) + _dot(a, f_lo)) + sign * (_dot(a, b_hi) + _dot(a, b_lo))) * wgt
        kq = ((_dot(sn, f_hi) + _dot(sn, f_lo)) + sign * (_dot(sn, b_hi) + _dot(sn, b_lo))) * wgt
        kq_ref[rows, :] = jnp.where(row == 0, nyq, kq)


def _hy_filter(length, feats, w1, fb, ffreq, wmid, wout, a, sn):
    z, z_rev, dec, dec_rev = feats
    nj = HY_W // HY_CW
    full = lambda shape: pl.BlockSpec(shape, lambda l, j: (0,) * len(shape))
    chan = lambda shape: pl.BlockSpec(shape, lambda l, j: (0, j))
    per_layer = lambda shape: pl.BlockSpec((None,) + shape, lambda l, j: (l,) + (0,) * len(shape))
    out_spec = pl.BlockSpec((None, length, HY_CW), lambda l, j: (l, 0, j))
    return pl.pallas_call(
        _hyfilt_body,
        out_shape=(jax.ShapeDtypeStruct((DEPTH, length, HY_W), f32),
                   jax.ShapeDtypeStruct((DEPTH, length, HY_W), f32)),
        grid=(DEPTH, nj),
        in_specs=[full((length, 64)), full((length, 64)), chan((length, HY_CW)), chan((length, HY_CW)),
                  per_layer((64, HY_HID)), per_layer((3, HY_HID)), per_layer((3, HY_HID)),
                  per_layer((2, HY_HID, HY_HID)),
                  pl.BlockSpec((None, HY_HID, HY_CW), lambda l, j: (l, 0, j)),
                  pl.BlockSpec((None, HY_HID, HY_CW), lambda l, j: (l, 0, nj + j)),
                  full((length, length)), full((length, length))],
        out_specs=(out_spec, out_spec),
        scratch_shapes=[pltpu.VMEM((length, HY_HID), f32), pltpu.VMEM((length, HY_HID), f32)],
        compiler_params=_cp(("parallel", "arbitrary"), VMEM_BIG),
        name="hyena_filter",
    )(z, z_rev, dec, dec_rev, w1, fb, ffreq, wmid, wout, wout, a, sn)


def _hyconv_body(x0_ref, x1_ref, v_ref, cw_ref, cb_ref, skip_ref, a_ref, sn_ref, kp_ref, kq_ref, o_ref):
    length = x0_ref.shape[0]
    fc = min(HY_FC, length)
    row = lax.broadcasted_iota(jnp.int32, (length, HY_CW), 0)

    def short_conv(u_ref, part):
        u = u_ref[...].astype(f32)
        prev = jnp.where(row == 0, 0.0, pltpu.roll(u, 1, 0))
        nxt = jnp.where(row == length - 1, 0.0, pltpu.roll(u, length - 1, 0))
        w = cw_ref[part]
        return prev * w[0:1, :] + u * w[1:2, :] + nxt * w[2:3, :] + cb_ref[part]

    x0 = short_conv(x0_ref, 0)
    x1 = short_conv(x1_ref, 1)
    vv = short_conv(v_ref, 2) * x1
    vb = vv.astype(bf16)
    alt = _alternating((length, HY_CW))
    y = alt * (jnp.sum(alt * vv, axis=0, keepdims=True) * kq_ref[0:1, :])
    for c in range(length // fc):
        rows = slice(c * fc, (c + 1) * fc)
        pc = _dot(a_ref[rows, :], vb)
        qs = _dot(sn_ref[rows, :], vb)
        kp = kp_ref[rows, :]
        kq = kq_ref[rows, :]
        yp = (pc * kp - qs * kq).astype(bf16)
        yq = (pc * kq + qs * kp).astype(bf16)
        y = y + _dot(a_ref[:, rows], yp) + _dot(sn_ref[:, rows], yq)
    o_ref[...] = ((y + vv * skip_ref[...]) * x0).astype(bf16)


def _hy_conv(p, length, row_blk0, n_seq, conv_w, conv_b, skip, tabs, kp, kq):
    a, sn = tabs
    nj = HY_W // HY_CW
    c0 = C_HY // HY_CW

    def part(k):
        return pl.BlockSpec((length, HY_CW), lambda j, b: (row_blk0 + b, c0 + k * nj + j))

    chan3 = lambda shape: pl.BlockSpec(shape, lambda j, b: (0, 0, j))
    return pl.pallas_call(
        _hyconv_body,
        out_shape=jax.ShapeDtypeStruct((n_seq * length, HY_W), bf16),
        grid=(nj, n_seq),
        in_specs=[part(0), part(1), part(2),
                  chan3((3, 3, HY_CW)), chan3((3, 1, HY_CW)),
                  pl.BlockSpec((1, HY_CW), lambda j, b: (0, j)),
                  _resident((length, length)), _resident((length, length)),
                  pl.BlockSpec((length, HY_CW), lambda j, b: (0, j)),
                  pl.BlockSpec((length, HY_CW), lambda j, b: (0, j))],
        out_specs=pl.BlockSpec((length, HY_CW), lambda j, b: (b, j)),
        compiler_params=_cp(("parallel", "parallel"), VMEM_BIG),
        name="hyena_conv",
    )(p, p, p, conv_w, conv_b, skip, a, sn, kp, kq)


def _merge_body(og_ref, oh_ref, on_ref, om_ref, gate_ref, wb_ref, wo_ref, h_ref, m_ref, o_ref):
    merged = None
    for i, o in enumerate((og_ref, oh_ref, on_ref, om_ref)):
        g = jax.nn.sigmoid(gate_ref[:, i * D:(i + 1) * D].astype(f32))
        term = g * _dot(o[...], wb_ref[i])
        merged = term if merged is None else merged + term
    mix = _dot(merged.astype(bf16), wo_ref[...])
    o_ref[...] = h_ref[...] + m_ref[...] * mix


def _merge(outs, p, w_branch, w_out, h, mgate, n_rows, n_batch):
    ospec = pl.BlockSpec((TM, BW), lambda i: (i, 0))
    return pl.pallas_call(
        _merge_body,
        out_shape=jax.ShapeDtypeStruct((n_rows, D), f32),
        grid=(n_rows // TM,),
        in_specs=[ospec, ospec, ospec, ospec,
                  pl.BlockSpec((TM, 4 * D), lambda i: (i, C_GATE // (4 * D))),
                  _resident((4, BW, D)), _resident((D, D)),
                  pl.BlockSpec((TM, D), lambda i: (i, 0)), _mod_spec(n_batch)],
        out_specs=pl.BlockSpec((TM, D), lambda i: (i, 0)),
        compiler_params=_cp(("parallel",), VMEM_BIG),
        name="merge",
    )(*outs, p, w_branch, w_out, h, mgate)


def _final_body(h_ref, g_ref, o_ref):
    o_ref[...] = _rms(h_ref[...]) * g_ref[...]


def _final_norm(h, gain, n_rows):
    return pl.pallas_call(
        _final_body,
        out_shape=jax.ShapeDtypeStruct((n_rows, D), f32),
        grid=(n_rows // TM,),
        in_specs=[pl.BlockSpec((TM, D), lambda i: (i, 0)), pl.BlockSpec((1, D), lambda i: (0, 0))],
        out_specs=pl.BlockSpec((TM, D), lambda i: (i, 0)),
        compiler_params=_cp(("parallel",)),
        name="final_norm",
    )(h, gain)


def _rope_table(half, n_heads):
    freqs = ROPE_THETA ** (-jnp.arange(half, dtype=f32) / half)
    pos = jnp.arange(S)
    ar = (pos // GRID_W).astype(f32)[:, None] * freqs
    ac = (pos % GRID_W).astype(f32)[:, None] * freqs
    cos_h = jnp.concatenate([jnp.cos(ar), jnp.cos(ar), jnp.cos(ac), jnp.cos(ac)], axis=-1)
    sin_h = jnp.concatenate([-jnp.sin(ar), jnp.sin(ar), -jnp.sin(ac), jnp.sin(ac)], axis=-1)
    return jnp.tile(cos_h, (1, n_heads)), jnp.tile(sin_h, (1, n_heads))


def _with_identity_rows(cos, sin, n):
    return (jnp.concatenate([cos, jnp.ones((n, cos.shape[1]), f32)], axis=0),
            jnp.concatenate([sin, jnp.zeros((n, sin.shape[1]), f32)], axis=0))


def _mla_rope_table():
    cos, sin = _rope_table(8, 1)
    pad = MLA_SLOT - MLA_NOPE - MLA_ROPE
    return (jnp.concatenate([jnp.ones((S, MLA_NOPE), f32), cos, jnp.ones((S, pad), f32)], axis=1),
            jnp.concatenate([jnp.zeros((S, MLA_NOPE), f32), sin, jnp.zeros((S, pad), f32)], axis=1))


def _pack_w_in(w_in):
    z = lambda n: jnp.zeros(w_in.shape[:-1] + (n,), w_in.dtype)
    sl = lambda a, b: w_in[..., a:b]
    return jnp.concatenate([
        sl(4512, 8608),
        sl(768, 2304),
        sl(0, 512),
        sl(2304, 2816) * (HEAD ** -0.5 * LOG2E),
        sl(2816, 3840),
        sl(4224, 4480),
        sl(512, 768),
        sl(3840, 4224), z(128),
        z(MLA_NOPE), sl(4480, 4512), z(MLA_SLOT - MLA_NOPE - MLA_ROPE),
    ], axis=-1).astype(bf16)


def _pack_wq(wq_b):
    w = wq_b.reshape(DEPTH, MLA_QR, MLA_H, MLA_NOPE + MLA_ROPE)
    w = jnp.pad(w, ((0, 0), (0, 512 - MLA_QR), (0, 0), (0, MLA_SLOT - MLA_NOPE - MLA_ROPE)))
    return w.reshape(DEPTH, 512, MLA_H * MLA_SLOT).astype(bf16)


def _pack_wkv(wkv_b):
    w = wkv_b.reshape(DEPTH, MLA_KVR, MLA_H, MLA_NOPE + MLA_V)
    slots = lambda a: jnp.pad(a, ((0, 0), (0, 0), (0, 0), (0, MLA_SLOT - a.shape[-1]))).reshape(DEPTH, MLA_KVR, -1)
    return jnp.concatenate([slots(w[..., :MLA_NOPE]), slots(w[..., MLA_NOPE:])], axis=-1).astype(bf16)


def kernel(x, c, ctx, c_ctx, ada_w, ada_b, ffn1_up, ffn1_down, w_in, gqa_q_norm, gqa_k_norm, hy_conv_w, hy_conv_b, hy_f_w1, hy_f_b, hy_f_freq, hy_f_w_mid, hy_f_w_out, hy_skip, na_rpb, mla_q_norm, mla_kv_norm, mla_wq_b, mla_wkv_b, w_branch, w_out, ffn2_up, ffn2_down, final_norm):
    nb = x.shape[0]
    rows_lat = nb * S
    rows_all = rows_lat + nb * NCTX

    cond = jnp.concatenate([c, c_ctx[None], jnp.zeros((16 - nb - 1, D), f32)], axis=0)
    mod = _adaln(cond, ada_w, ada_b)[:, :nb + 1].reshape(DEPTH, nb + 1, N_MOD, 1, D)
    mods = lambda l, k: mod[l, :, k]

    w1u, w1d = ffn1_up.astype(bf16), ffn1_down.astype(bf16)
    w2u, w2d = ffn2_up.astype(bf16), ffn2_down.astype(bf16)
    w_in_p = _pack_w_in(w_in)
    wq_p, wkv_p = _pack_wq(mla_wq_b), _pack_wkv(mla_wkv_b)
    wb, wo = w_branch.astype(bf16), w_out.astype(bf16)

    gq_tab = _with_identity_rows(*_rope_table(16, GQA_H), NCTX)
    gk_tab = _rope_table(16, GQA_KVH)
    mk_tab = _mla_rope_table()
    mq_tab = _with_identity_rows(jnp.tile(mk_tab[0], (1, MLA_H)), jnp.tile(mk_tab[1], (1, MLA_H)), NCTX)
    lane = jnp.arange(BW)
    gmat = (lane[:, None] // HEAD == lane[None, :] // HEAD).astype(bf16)

    hy_w1 = jnp.pad(hy_f_w1, ((0, 0), (0, 64 - HY_EMB), (0, 0)))
    dft_lat, dft_ctx = _dft_tables(S), _dft_tables(NCTX)
    kp_lat, kq_lat = _hy_filter(S, _filter_features(S), hy_w1, hy_f_b, hy_f_freq, hy_f_w_mid, hy_f_w_out,
                                dft_lat[0], dft_lat[1])
    kp_ctx, kq_ctx = _hy_filter(NCTX, _filter_features(NCTX), hy_w1, hy_f_b, hy_f_freq, hy_f_w_mid, hy_f_w_out,
                                dft_ctx[0], dft_ctx[1])
    conv_w = hy_conv_w.reshape(DEPTH, 3, 3, HY_W).transpose(0, 2, 1, 3)
    conv_b = hy_conv_b.reshape(DEPTH, 3, 1, HY_W)

    h = jnp.concatenate([x.reshape(rows_lat, D), ctx.reshape(nb * NCTX, D)], axis=0)
    for l in range(DEPTH):
        with_ctx = l < DEPTH - 1
        n_out = rows_all if with_ctx else rows_lat
        h = _ffn(h, mods(l, 0), mods(l, 1), mods(l, 2), w1u[l], w1d[l], rows_all, nb)
        p = _proj(h, mods(l, 3), mods(l, 4), w_in_p[l], rows_all, nb)
        o_gqa = _gqa(p, gq_tab + gk_tab, jnp.tile(gqa_q_norm[l], GQA_H)[None], jnp.tile(gqa_k_norm[l], GQA_KVH)[None],
                     gmat, nb, with_ctx)
        o_hy = _hy_conv(p, S, 0, nb, conv_w[l], conv_b[l], hy_skip[l][None], dft_lat, kp_lat[l], kq_lat[l])
        if with_ctx:
            o_hy_c = _hy_conv(p, NCTX, rows_lat // NCTX, nb, conv_w[l], conv_b[l], hy_skip[l][None], dft_ctx,
                              kp_ctx[l], kq_ctx[l])
            o_hy = jnp.concatenate([o_hy, o_hy_c], axis=0)
        o_na = _na(p, _na_bias(na_rpb[l]), nb, with_ctx)
        o_mla = _mla(p, mq_tab + mk_tab, jnp.pad(mla_q_norm[l], (0, 512 - MLA_QR))[None], mla_kv_norm[l][None],
                     wq_p[l], wkv_p[l], nb, with_ctx)
        h = _merge((o_gqa, o_hy, o_na, o_mla), p, wb[l], wo[l], h, mods(l, 5), n_out, nb)
        h = _ffn(h, mods(l, 6), mods(l, 7), mods(l, 8), w2u[l], w2d[l], n_out, nb)
    out = _final_norm(h, final_norm[None], rows_lat)
    return out.reshape(nb, S, D)
```

```python
import functools
import math

import numpy as np
import jax
import jax.numpy as jnp
from jax import lax
from jax.experimental import pallas as pl
from jax.experimental.pallas import tpu as pltpu

f32 = jnp.float32
bf16 = jnp.bfloat16

D = 1024
S = 2048
DEPTH = 4
GRID_W = 64
NCTX = 256
HEAD = 64
ROPE_THETA = 10000.0
RMS_EPS = 1e-6
N_MOD = 9
F = 2816
BW = 512
GQA_H, GQA_KVH, GQA_GROUP = 8, 2, 4
NA_H, NA_ROWS, NA_COLS = 8, 8, 16
MLA_H, MLA_NOPE, MLA_ROPE, MLA_V = 8, 64, 32, 64
MLA_QR, MLA_KVR = 384, 256
MLA_SCALE = (MLA_NOPE + MLA_ROPE) ** -0.5
MLA_SLOT = 128
MLA_LOCKSTEP = 4
LOG2E = math.log2(math.e)
HY_W = 512
HY_BANDS = 16
HY_EMB = 1 + 2 * HY_BANDS
HY_HID = 64
HY_MIN_DECAY = math.log(1e-2) / 1.5
HY_MAX_DECAY = math.log(1e-2) / 0.3
NEG = -1e30

C_GATE = 0
C_HY = 4096
C_GQ = 5632
C_NQ, C_NK, C_NV = 6144, 6656, 7168
C_MKVA = 7680
C_GK, C_GV = 7936, 8064
C_MQA = 8192
C_MKR = 8704
PW = 8832

TQ = 256
TM = 512
VMEM_BIG = 56 * 1024 * 1024


def _cp(sem, vmem=None):
    return pltpu.CompilerParams(dimension_semantics=sem, vmem_limit_bytes=vmem)


def _resident(shape):
    nd = len(shape)
    return pl.BlockSpec(shape, lambda *_: (0,) * nd, pipeline_mode=pl.Buffered(1))


def _dot(a, b):
    return jnp.dot(a, b, preferred_element_type=f32)


def _dot_t(a, b):
    return lax.dot_general(a, b, (((1,), (1,)), ((), ())), preferred_element_type=f32)


def _rms(x):
    return x * lax.rsqrt(jnp.mean(x * x, axis=-1, keepdims=True) + RMS_EPS)


def _group_of_tile(i, n_batch):
    return jnp.minimum((i * TM) // S, n_batch)


def _mod_spec(n_batch):
    return pl.BlockSpec((None, 1, D), lambda i: (_group_of_tile(i, n_batch), 0, 0))


def _ada_body(c_ref, w_ref, b_ref, o_ref):
    x = c_ref[...]
    xs = (x * jax.nn.sigmoid(x)).astype(bf16)
    o_ref[...] = _dot(xs, w_ref[...].astype(bf16)) + b_ref[...]


def _adaln(cond, ada_w, ada_b):
    tn = 1024
    return pl.pallas_call(
        _ada_body,
        out_shape=jax.ShapeDtypeStruct((DEPTH, 16, N_MOD * D), f32),
        grid=(DEPTH, N_MOD * D // tn),
        in_specs=[pl.BlockSpec((16, D), lambda l, j: (0, 0)),
                  pl.BlockSpec((None, D, tn), lambda l, j: (l, 0, j)),
                  pl.BlockSpec((None, 1, tn), lambda l, j: (l, 0, j))],
        out_specs=pl.BlockSpec((None, 16, tn), lambda l, j: (l, 0, j)),
        compiler_params=_cp(("parallel", "parallel")),
        name="adaln",
    )(cond, ada_w, ada_b.reshape(DEPTH, 1, N_MOD * D))


def _ffn_body(h_ref, sh_ref, sc_ref, gt_ref, wup_ref, wdn_ref, o_ref):
    h = h_ref[...]
    xm = (_rms(h) * (1.0 + sc_ref[...]) + sh_ref[...]).astype(bf16)
    a = _dot(xm, wup_ref[:, :F])
    g = _dot(xm, wup_ref[:, F:])
    mid = (a * jax.nn.sigmoid(a) * g).astype(bf16)
    y = _dot(mid, wdn_ref[...])
    o_ref[...] = h + (0.5 * gt_ref[...]) * y


def _ffn(h, shift, scale, gate, w_up, w_down, n_rows, n_batch):
    tm = 256
    grp = lambda i: (jnp.minimum((i * tm) // S, n_batch), 0, 0)
    mspec = pl.BlockSpec((None, 1, D), grp)
    return pl.pallas_call(
        _ffn_body,
        out_shape=jax.ShapeDtypeStruct((n_rows, D), f32),
        grid=(n_rows // tm,),
        in_specs=[pl.BlockSpec((tm, D), lambda i: (i, 0)), mspec, mspec, mspec,
                  _resident((D, 2 * F)), _resident((F, D))],
        out_specs=pl.BlockSpec((tm, D), lambda i: (i, 0)),
        compiler_params=_cp(("parallel",), VMEM_BIG),
        name="ffn",
    )(h, shift, scale, gate, w_up, w_down)


PROJ_CHUNK = PW // 3


def _proj_body(h_ref, sh_ref, sc_ref, w_ref, o_ref):
    xm = (_rms(h_ref[...]) * (1.0 + sc_ref[...]) + sh_ref[...]).astype(bf16)
    for j in range(0, PW, PROJ_CHUNK):
        o_ref[:, j:j + PROJ_CHUNK] = _dot(xm, w_ref[:, j:j + PROJ_CHUNK]).astype(bf16)


def _proj(h, shift, scale, w, n_rows, n_batch):
    return pl.pallas_call(
        _proj_body,
        out_shape=jax.ShapeDtypeStruct((n_rows, PW), bf16),
        grid=(n_rows // TM,),
        in_specs=[pl.BlockSpec((TM, D), lambda i: (i, 0)), _mod_spec(n_batch), _mod_spec(n_batch),
                  _resident((D, PW))],
        out_specs=pl.BlockSpec((TM, PW), lambda i: (i, 0)),
        compiler_params=_cp(("parallel",), VMEM_BIG),
        name="proj",
    )(h, shift, scale, w)


def _group_mean_sq(x, gmat, width):
    x2 = x * x
    hi = x2.astype(bf16)
    lo = (x2 - hi.astype(f32)).astype(bf16)
    return (_dot(hi, gmat) + _dot(lo, gmat)) * (1.0 / width)


def _swap_halves(x, half):
    n = x.shape[-1]
    lane = lax.broadcasted_iota(jnp.int32, x.shape, 1)
    first = (lane % (2 * half)) < half
    return jnp.where(first, pltpu.roll(x, n - half, 1), pltpu.roll(x, half, 1))


def _rope(x, cos, sin_signed, half):
    return x * cos + _swap_halves(x, half) * sin_signed


def _softmax_pv(s, v):
    m = jnp.max(s, axis=-1, keepdims=True)
    e = jnp.exp2(s - m)
    l = jnp.sum(e, axis=-1, keepdims=True)
    return _dot(e.astype(bf16), v) / l


KCH = 256
VT_ROWS = 80


def _attend_key_major(get_k, get_vt, q_ts, n, dv):
    nh, nc = len(q_ts), n // KCH
    tq = q_ts[0].shape[1]
    m = [jnp.full((1, tq), NEG, f32) for _ in range(nh)]
    o = [jnp.zeros((VT_ROWS, tq), f32) for _ in range(nh)]
    s_next = [_dot(get_k(i, 0), q_ts[i]) for i in range(nh)]
    for c in range(nc):
        s_cur = s_next
        if c + 1 < nc:
            s_next = [_dot(get_k(i, c + 1), q_ts[i]) for i in range(nh)]
        for i in range(nh):
            m_new = jnp.maximum(m[i], jnp.max(s_cur[i], axis=0, keepdims=True))
            e = jnp.exp2(s_cur[i] - m_new).astype(bf16)
            o[i] = o[i] * jnp.exp2(m[i] - m_new) + _dot(get_vt(i, c), e)
            m[i] = m_new
    return [oi[:dv, :] / oi[dv:dv + 1, :] for oi in o]


def _q_row_block(n_batch):
    nq = S // TQ
    return lambda b, qi: jnp.where(qi < nq, b * nq + qi, n_batch * nq + b)


def _gqa_body(q_ref, kl_ref, kc_ref, vl_ref, vc_ref, qcos_ref, qsin_ref, kcos_ref, ksin_ref,
              qg_ref, kg_ref, gm_ref, o_ref, k_s, vt_s, ot_s):
    qi = pl.program_id(1)
    nq = S // TQ

    @pl.when(qi == 0)
    def _():
        gmk = gm_ref[:2 * HEAD, :2 * HEAD]
        kl = kl_ref[...].astype(f32)
        kn = kl * lax.rsqrt(_group_mean_sq(kl, gmk, HEAD) + RMS_EPS) * kg_ref[...]
        kn = _rope(kn, kcos_ref[...], ksin_ref[...], 16).astype(bf16)
        kc = kc_ref[...].astype(f32)
        kcn = (kc * lax.rsqrt(_group_mean_sq(kc, gmk, HEAD) + RMS_EPS) * kg_ref[...]).astype(bf16)
        vl_t = vl_ref[...].astype(f32).T.astype(bf16)
        vc_t = vc_ref[...].astype(f32).T.astype(bf16)
        for g in range(GQA_KVH):
            k_s[g, :S, :] = kn[:, g * HEAD:(g + 1) * HEAD]
            k_s[g, S:, :] = kcn[:, g * HEAD:(g + 1) * HEAD]
            vt_s[g, :HEAD, :S] = vl_t[g * HEAD:(g + 1) * HEAD, :]
            vt_s[g, :HEAD, S:] = vc_t[g * HEAD:(g + 1) * HEAD, :]
            vt_s[g, HEAD:, :] = jnp.ones((VT_ROWS - HEAD, S + NCTX), bf16)

    q = q_ref[...].astype(f32)
    qn = q * lax.rsqrt(_group_mean_sq(q, gm_ref[...], HEAD) + RMS_EPS) * qg_ref[...]
    qr = _rope(qn, qcos_ref[...], qsin_ref[...], 16) * (HEAD ** -0.5 * LOG2E)
    q_t = qr.T.astype(bf16)

    def attend(lo):
        for g in range(GQA_KVH):
            heads = range(g * GQA_GROUP, (g + 1) * GQA_GROUP)
            outs = _attend_key_major(
                lambda i, c: k_s[g, lo + c * KCH:lo + (c + 1) * KCH, :],
                lambda i, c: vt_s[g, :, lo + c * KCH:lo + (c + 1) * KCH],
                [q_t[h * HEAD:(h + 1) * HEAD, :] for h in heads], S + NCTX - lo, HEAD)
            for h, o_t in zip(heads, outs):
                ot_s[h * HEAD:(h + 1) * HEAD, :] = o_t
        o_ref[...] = ot_s[...].T.astype(bf16)

    @pl.when(qi < nq)
    def _():
        attend(0)

    @pl.when(qi >= nq)
    def _():
        attend(S)


def _gqa(p, tabs, q_gain, k_gain, gmat, n_batch, with_ctx):
    nq = S // TQ
    n_rows = n_batch * S + (n_batch * NCTX if with_ctx else 0)
    qblk = _q_row_block(n_batch)
    ctx_blk = lambda b, qi: n_batch * nq + b
    qcos, qsin, kcos, ksin = tabs
    return pl.pallas_call(
        _gqa_body,
        out_shape=jax.ShapeDtypeStruct((n_rows, BW), bf16),
        grid=(n_batch, nq + (1 if with_ctx else 0)),
        in_specs=[
            pl.BlockSpec((TQ, BW), lambda b, qi: (qblk(b, qi), C_GQ // BW)),
            pl.BlockSpec((S, 128), lambda b, qi: (b, C_GK // 128)),
            pl.BlockSpec((NCTX, 128), lambda b, qi: (ctx_blk(b, qi), C_GK // 128)),
            pl.BlockSpec((S, 128), lambda b, qi: (b, C_GV // 128)),
            pl.BlockSpec((NCTX, 128), lambda b, qi: (ctx_blk(b, qi), C_GV // 128)),
            pl.BlockSpec((TQ, BW), lambda b, qi: (jnp.minimum(qi, nq), 0)),
            pl.BlockSpec((TQ, BW), lambda b, qi: (jnp.minimum(qi, nq), 0)),
            pl.BlockSpec((S, 128), lambda b, qi: (0, 0)),
            pl.BlockSpec((S, 128), lambda b, qi: (0, 0)),
            pl.BlockSpec((1, BW), lambda b, qi: (0, 0)),
            pl.BlockSpec((1, 128), lambda b, qi: (0, 0)),
            pl.BlockSpec((BW, BW), lambda b, qi: (0, 0)),
        ],
        out_specs=pl.BlockSpec((TQ, BW), lambda b, qi: (qblk(b, qi), 0)),
        scratch_shapes=[pltpu.VMEM((GQA_KVH, S + NCTX, HEAD), bf16),
                        pltpu.VMEM((GQA_KVH, VT_ROWS, S + NCTX), bf16),
                        pltpu.VMEM((BW, TQ), f32)],
        compiler_params=_cp(("parallel", "arbitrary"), VMEM_BIG),
        name="gqa",
    )(p, p, p, p, p, qcos, qsin, kcos, ksin, q_gain, k_gain, gmat)


def _mla_body(qa_ref, kval_ref, kvac_ref, krl_ref, krc_ref, qcos_ref, qsin_ref, kcos_ref, ksin_ref,
              qg_ref, kvg_ref, wq_ref, wkv_ref, o_ref, k_s, vt_s, ot_s):
    qi = pl.program_id(1)
    nq = S // TQ
    slot = MLA_SLOT

    @pl.when(qi == 0)
    def _():
        lane = lax.broadcasted_iota(jnp.int32, (1, 2 * slot), 1)
        ones_hi = jnp.where(lane % slot >= MLA_V, 1.0, 0.0)

        def kv(ref, kr, lo, n):
            a = (_rms(ref[...].astype(f32)) * kvg_ref[...]).astype(bf16)
            for hp in range(MLA_H // 2):
                kk = _dot(a, wkv_ref[:, hp * 2 * slot:(hp + 1) * 2 * slot])
                vv = _dot(a, wkv_ref[:, (MLA_H + hp * 2) * slot:(MLA_H + hp * 2 + 2) * slot])
                vv_t = (vv + ones_hi).T.astype(bf16)
                for i in range(2):
                    k_s[2 * hp + i, lo:lo + n, :] = (kk[:, i * slot:(i + 1) * slot] + kr).astype(bf16)
                    vt_s[2 * hp + i, :, lo:lo + n] = vv_t[i * slot:i * slot + VT_ROWS, :]

        kv(kval_ref, _rope(krl_ref[...].astype(f32), kcos_ref[...], ksin_ref[...], 8), 0, S)
        kv(kvac_ref, krc_ref[...].astype(f32), S, NCTX)

    qa = qa_ref[...].astype(f32)
    qan = qa * lax.rsqrt(jnp.sum(qa * qa, axis=-1, keepdims=True) * (1.0 / MLA_QR) + RMS_EPS)
    qh = _dot((qan * qg_ref[...]).astype(bf16), wq_ref[...])
    q_t = (_rope(qh, qcos_ref[...], qsin_ref[...], 8) * (MLA_SCALE * LOG2E)).T.astype(bf16)

    def attend(lo):
        for h0 in range(0, MLA_H, MLA_LOCKSTEP):
            heads = range(h0, h0 + MLA_LOCKSTEP)
            outs = _attend_key_major(
                lambda i, c: k_s[h0 + i, lo + c * KCH:lo + (c + 1) * KCH, :],
                lambda i, c: vt_s[h0 + i, :, lo + c * KCH:lo + (c + 1) * KCH],
                [q_t[h * slot:(h + 1) * slot, :] for h in heads], S + NCTX - lo, MLA_V)
            for h, o_t in zip(heads, outs):
                ot_s[h * MLA_V:(h + 1) * MLA_V, :] = o_t
        o_ref[...] = ot_s[...].T.astype(bf16)

    @pl.when(qi < nq)
    def _():
        attend(0)

    @pl.when(qi >= nq)
    def _():
        attend(S)


def _mla(p, tabs, q_gain, kv_gain, wq, wkv, n_batch, with_ctx):
    nq = S // TQ
    n_rows = n_batch * S + (n_batch * NCTX if with_ctx else 0)
    qblk = _q_row_block(n_batch)
    ctx_blk = lambda b, qi: n_batch * nq + b
    qcos, qsin, kcos, ksin = tabs
    rw = MLA_H * MLA_SLOT
    return pl.pallas_call(
        _mla_body,
        out_shape=jax.ShapeDtypeStruct((n_rows, BW), bf16),
        grid=(n_batch, nq + (1 if with_ctx else 0)),
        in_specs=[
            pl.BlockSpec((TQ, 512), lambda b, qi: (qblk(b, qi), C_MQA // 512)),
            pl.BlockSpec((S, MLA_KVR), lambda b, qi: (b, C_MKVA // MLA_KVR)),
            pl.BlockSpec((NCTX, MLA_KVR), lambda b, qi: (ctx_blk(b, qi), C_MKVA // MLA_KVR)),
            pl.BlockSpec((S, 128), lambda b, qi: (b, C_MKR // 128)),
            pl.BlockSpec((NCTX, 128), lambda b, qi: (ctx_blk(b, qi), C_MKR // 128)),
            pl.BlockSpec((TQ, rw), lambda b, qi: (jnp.minimum(qi, nq), 0)),
            pl.BlockSpec((TQ, rw), lambda b, qi: (jnp.minimum(qi, nq), 0)),
            pl.BlockSpec((S, 128), lambda b, qi: (0, 0)),
            pl.BlockSpec((S, 128), lambda b, qi: (0, 0)),
            pl.BlockSpec((1, 512), lambda b, qi: (0, 0)),
            pl.BlockSpec((1, MLA_KVR), lambda b, qi: (0, 0)),
            pl.BlockSpec((512, MLA_H * MLA_SLOT), lambda b, qi: (0, 0)),
            pl.BlockSpec((MLA_KVR, 2 * MLA_H * MLA_SLOT), lambda b, qi: (0, 0)),
        ],
        out_specs=pl.BlockSpec((TQ, BW), lambda b, qi: (qblk(b, qi), 0)),
        scratch_shapes=[pltpu.VMEM((MLA_H, S + NCTX, MLA_SLOT), bf16),
                        pltpu.VMEM((MLA_H, VT_ROWS, S + NCTX), bf16),
                        pltpu.VMEM((BW, TQ), f32)],
        compiler_params=_cp(("parallel", "arbitrary"), VMEM_BIG),
        name="mla",
    )(p, p, p, p, p, qcos, qsin, kcos, ksin, q_gain, kv_gain, wq, wkv)


def _na_body(q_ref, k0_ref, k1_ref, k2_ref, kc_ref, v0_ref, v1_ref, v2_ref, vc_ref, bias_ref, o_ref):
    t = pl.program_id(0)
    nt = S // TQ
    q = q_ref[...]

    @pl.when(t < nt)
    def _():
        kcat = jnp.concatenate([k0_ref[...], k1_ref[...], k2_ref[...], kc_ref[...]], axis=0)
        vcat = jnp.concatenate([v0_ref[...], v1_ref[...], v2_ref[...], vc_ref[...]], axis=0)
        for h in range(NA_H):
            hs = slice(h * HEAD, (h + 1) * HEAD)
            s = _dot_t(q[:, hs], kcat[:, hs]) + bias_ref[h]
            o_ref[:, hs] = _softmax_pv(s, vcat[:, hs]).astype(bf16)

    @pl.when(t >= nt)
    def _():
        for h in range(NA_H):
            hs = slice(h * HEAD, (h + 1) * HEAD)
            s = _dot_t(q[:, hs], kc_ref[:, hs])
            o_ref[:, hs] = _softmax_pv(s, vc_ref[:, hs]).astype(bf16)


def _na(p, bias, n_batch, with_ctx):
    nt = S // TQ
    n_rows = n_batch * S + (n_batch * NCTX if with_ctx else 0)
    qblk = lambda t, b: jnp.where(t < nt, b * nt + t, n_batch * nt + b)
    ctx_blk = lambda t, b: n_batch * nt + b
    centre = lambda t: jnp.clip(t, 1, nt - 2)
    variant = lambda t: jnp.where(t == 0, 0, jnp.where(t >= nt - 1, 2, 1))

    def win(col, d):
        return pl.BlockSpec((TQ, BW), lambda t, b: (b * nt + centre(t) + d, col // BW))

    return pl.pallas_call(
        _na_body,
        out_shape=jax.ShapeDtypeStruct((n_rows, BW), bf16),
        grid=(nt + (1 if with_ctx else 0), n_batch),
        in_specs=[
            pl.BlockSpec((TQ, BW), lambda t, b: (qblk(t, b), C_NQ // BW)),
            win(C_NK, -1), win(C_NK, 0), win(C_NK, 1),
            pl.BlockSpec((NCTX, BW), lambda t, b: (ctx_blk(t, b), C_NK // BW)),
            win(C_NV, -1), win(C_NV, 0), win(C_NV, 1),
            pl.BlockSpec((NCTX, BW), lambda t, b: (ctx_blk(t, b), C_NV // BW)),
            pl.BlockSpec((None, NA_H, TQ, 4 * TQ), lambda t, b: (variant(t), 0, 0, 0)),
        ],
        out_specs=pl.BlockSpec((TQ, BW), lambda t, b: (qblk(t, b), 0)),
        compiler_params=_cp(("arbitrary", "arbitrary"), VMEM_BIG),
        name="na",
    )(p, p, p, p, p, p, p, p, p, bias)


def _na_bias(rpb):
    rows_per_tile = TQ // GRID_W
    n_rows = S // GRID_W
    qc = np.arange(GRID_W)[:, None]
    kc = np.arange(GRID_W)[None, :]
    col_start = np.clip(qc - NA_COLS // 2, 0, GRID_W - NA_COLS)
    col_ok = (kc >= col_start) & (kc < col_start + NA_COLS)
    cidx = np.clip(kc - qc + NA_COLS - 1, 0, 2 * NA_COLS - 2)
    onehot = (np.arange(2 * NA_COLS - 1)[:, None, None] == cidx[None]).astype(np.float32)
    blocks = jnp.einsum('lhrc,cqk->lhrqk', rpb.astype(f32), jnp.asarray(onehot), precision=lax.Precision.HIGHEST)
    blocks = jnp.where(jnp.asarray(col_ok), blocks * LOG2E, NEG)
    masked = jnp.full((DEPTH, NA_H, GRID_W, GRID_W), NEG, f32)
    out = []
    for t in (0, 1, S // TQ - 1):
        centre = min(max(t, 1), S // TQ - 2)
        tile_rows = []
        for qoff in range(rows_per_tile):
            r = rows_per_tile * t + qoff
            start = min(max(r - NA_ROWS // 2, 0), n_rows - NA_ROWS)
            row = []
            for koff in range(3 * rows_per_tile):
                krow = rows_per_tile * (centre - 1) + koff
                in_band = start <= krow < start + NA_ROWS
                row.append(blocks[:, :, krow - r + NA_ROWS - 1] if in_band else masked)
            row.append(jnp.zeros((DEPTH, NA_H, GRID_W, NCTX), f32))
            tile_rows.append(jnp.concatenate(row, axis=-1))
        out.append(jnp.concatenate(tile_rows, axis=-2))
    return jnp.stack(out, axis=1)


def _dft_tables(length):
    n = 2 * length
    f = jnp.arange(length, dtype=jnp.int32)[:, None]
    s = jnp.arange(length, dtype=jnp.int32)[None, :]
    ang = ((f * s) % n).astype(f32) * (2.0 * math.pi / n)
    return jnp.cos(ang).astype(bf16), jnp.sin(ang).astype(bf16)


def _filter_features(length):
    t = jnp.linspace(0.0, 1.0, length, dtype=f32)[:, None]
    w = (2.0 * math.pi / length) * jnp.arange(length, dtype=f32)[:, None]
    fb = jnp.linspace(1e-4, HY_BANDS - 1, HY_BANDS, dtype=f32)[None, :]
    z = jnp.concatenate([t, jnp.cos(fb * w), -jnp.sin(fb * w)], axis=-1)
    deltas = jnp.abs(jnp.linspace(HY_MIN_DECAY, HY_MAX_DECAY, HY_W, dtype=f32))
    dec = jnp.exp(-t * deltas)
    z = jnp.pad(z, ((0, 0), (0, 64 - HY_EMB)))
    z_rev = jnp.concatenate([z[:1], z[:0:-1]], axis=0)
    dec_rev = jnp.concatenate([jnp.zeros((1, HY_W), f32), dec[:0:-1]], axis=0)
    return z, z_rev, dec, dec_rev


HY_CW = 256
HY_FC = 512


def _alternating(shape):
    row = lax.broadcasted_iota(jnp.int32, shape, 0)
    return jnp.where(row % 2 == 0, 1.0, -1.0)


def _hyfilt_body(z_ref, zr_ref, dec_ref, decr_ref, w1_ref, b_ref, fr_ref, wm_ref, wof_ref, wob_ref, a_ref, sn_ref,
                 kp_ref, kq_ref, hf_s, hb_s):
    length = z_ref.shape[0]
    fc = min(HY_FC, length)

    def hidden(z):
        hdn = jnp.sin(fr_ref[0:1, :] * (_dot(z, w1_ref[...]) + b_ref[0:1, :]))
        hdn = jnp.sin(fr_ref[1:2, :] * (_dot(hdn, wm_ref[0]) + b_ref[1:2, :]))
        return jnp.sin(fr_ref[2:3, :] * (_dot(hdn, wm_ref[1]) + b_ref[2:3, :]))

    @pl.when(pl.program_id(1) == 0)
    def _():
        hf_s[...] = hidden(z_ref[...])
        hb_s[...] = hidden(zr_ref[...])

    ff = _dot(hf_s[...], wof_ref[...]) * dec_ref[...]
    fb = _dot(hb_s[...], wob_ref[...]) * decr_ref[...]
    norm = jnp.sum(jnp.abs(ff), axis=0, keepdims=True) + jnp.sum(jnp.abs(fb), axis=0, keepdims=True)
    ff = ff / norm
    fb = fb / norm
    inv_n = 1.0 / (2 * length)
    nyq = jnp.sum(_alternating((length, HY_CW)) * (ff + fb), axis=0, keepdims=True) * inv_n
    ff_b = ff.astype(bf16)
    fb_b = fb.astype(bf16)
    sign = _alternating((fc, HY_CW))
    for c in range(length // fc):
        rows = slice(c * fc, (c + 1) * fc)
        a = a_ref[rows, :]
        sn = sn_ref[rows, :]
        row = lax.broadcasted_iota(jnp.int32, (fc, HY_CW), 0) + c * fc
        wgt = jnp.where(row == 0, inv_n, 2.0 * inv_n)
        kp_ref[rows, :] = (_dot(a, ff_b) + sign * _dot(a, fb_b)) * wgt
        kq = (_dot(sn, ff_b) + sign * _dot(sn, fb_b)) * wgt
        kq_ref[rows, :] = jnp.where(row == 0, nyq, kq)


def _hy_filter(length, feats, w1, fb, ffreq, wmid, wout, a, sn):
    z, z_rev, dec, dec_rev = feats
    nj = HY_W // HY_CW
    full = lambda shape: pl.BlockSpec(shape, lambda l, j: (0,) * len(shape))
    chan = lambda shape: pl.BlockSpec(shape, lambda l, j: (0, j))
    per_layer = lambda shape: pl.BlockSpec((None,) + shape, lambda l, j: (l,) + (0,) * len(shape))
    out_spec = pl.BlockSpec((None, length, HY_CW), lambda l, j: (l, 0, j))
    return pl.pallas_call(
        _hyfilt_body,
        out_shape=(jax.ShapeDtypeStruct((DEPTH, length, HY_W), f32),
                   jax.ShapeDtypeStruct((DEPTH, length, HY_W), f32)),
        grid=(DEPTH, nj),
        in_specs=[full((length, 64)), full((length, 64)), chan((length, HY_CW)), chan((length, HY_CW)),
                  per_layer((64, HY_HID)), per_layer((3, HY_HID)), per_layer((3, HY_HID)),
                  per_layer((2, HY_HID, HY_HID)),
                  pl.BlockSpec((None, HY_HID, HY_CW), lambda l, j: (l, 0, j)),
                  pl.BlockSpec((None, HY_HID, HY_CW), lambda l, j: (l, 0, nj + j)),
                  full((length, length)), full((length, length))],
        out_specs=(out_spec, out_spec),
        scratch_shapes=[pltpu.VMEM((length, HY_HID), f32), pltpu.VMEM((length, HY_HID), f32)],
        compiler_params=_cp(("parallel", "arbitrary"), VMEM_BIG),
        name="hyena_filter",
    )(z, z_rev, dec, dec_rev, w1, fb, ffreq, wmid, wout, wout, a, sn)


def _hyconv_body(x0_ref, x1_ref, v_ref, cw_ref, cb_ref, skip_ref, a_ref, sn_ref, kp_ref, kq_ref, o_ref):
    length = x0_ref.shape[0]
    fc = min(HY_FC, length)
    row = lax.broadcasted_iota(jnp.int32, (length, HY_CW), 0)

    def short_conv(u_ref, part):
        u = u_ref[...].astype(f32)
        prev = jnp.where(row == 0, 0.0, pltpu.roll(u, 1, 0))
        nxt = jnp.where(row == length - 1, 0.0, pltpu.roll(u, length - 1, 0))
        w = cw_ref[part]
        return prev * w[0:1, :] + u * w[1:2, :] + nxt * w[2:3, :] + cb_ref[part]

    x0 = short_conv(x0_ref, 0)
    x1 = short_conv(x1_ref, 1)
    vv = short_conv(v_ref, 2) * x1
    vb = vv.astype(bf16)
    alt = _alternating((length, HY_CW))
    y = alt * (jnp.sum(alt * vv, axis=0, keepdims=True) * kq_ref[0:1, :])
    for c in range(length // fc):
        rows = slice(c * fc, (c + 1) * fc)
        pc = _dot(a_ref[rows, :], vb)
        qs = _dot(sn_ref[rows, :], vb)
        kp = kp_ref[rows, :]
        kq = kq_ref[rows, :]
        yp = (pc * kp - qs * kq).astype(bf16)
        yq = (pc * kq + qs * kp).astype(bf16)
        y = y + _dot(a_ref[:, rows], yp) + _dot(sn_ref[:, rows], yq)
    o_ref[...] = ((y + vv * skip_ref[...]) * x0).astype(bf16)


def _hy_conv(p, length, row_blk0, n_seq, conv_w, conv_b, skip, tabs, kp, kq):
    a, sn = tabs
    nj = HY_W // HY_CW
    c0 = C_HY // HY_CW

    def part(k):
        return pl.BlockSpec((length, HY_CW), lambda j, b: (row_blk0 + b, c0 + k * nj + j))

    chan3 = lambda shape: pl.BlockSpec(shape, lambda j, b: (0, 0, j))
    return pl.pallas_call(
        _hyconv_body,
        out_shape=jax.ShapeDtypeStruct((n_seq * length, HY_W), bf16),
        grid=(nj, n_seq),
        in_specs=[part(0), part(1), part(2),
                  chan3((3, 3, HY_CW)), chan3((3, 1, HY_CW)),
                  pl.BlockSpec((1, HY_CW), lambda j, b: (0, j)),
                  _resident((length, length)), _resident((length, length)),
                  pl.BlockSpec((length, HY_CW), lambda j, b: (0, j)),
                  pl.BlockSpec((length, HY_CW), lambda j, b: (0, j))],
        out_specs=pl.BlockSpec((length, HY_CW), lambda j, b: (b, j)),
        compiler_params=_cp(("parallel", "parallel"), VMEM_BIG),
        name="hyena_conv",
    )(p, p, p, conv_w, conv_b, skip, a, sn, kp, kq)


def _merge_body(og_ref, oh_ref, on_ref, om_ref, gate_ref, wb_ref, wo_ref, h_ref, m_ref, o_ref):
    merged = None
    for i, o in enumerate((og_ref, oh_ref, on_ref, om_ref)):
        g = jax.nn.sigmoid(gate_ref[:, i * D:(i + 1) * D].astype(f32))
        term = g * _dot(o[...], wb_ref[i])
        merged = term if merged is None else merged + term
    mix = _dot(merged.astype(bf16), wo_ref[...])
    o_ref[...] = h_ref[...] + m_ref[...] * mix


def _merge(outs, p, w_branch, w_out, h, mgate, n_rows, n_batch):
    ospec = pl.BlockSpec((TM, BW), lambda i: (i, 0))
    return pl.pallas_call(
        _merge_body,
        out_shape=jax.ShapeDtypeStruct((n_rows, D), f32),
        grid=(n_rows // TM,),
        in_specs=[ospec, ospec, ospec, ospec,
                  pl.BlockSpec((TM, 4 * D), lambda i: (i, C_GATE // (4 * D))),
                  _resident((4, BW, D)), _resident((D, D)),
                  pl.BlockSpec((TM, D), lambda i: (i, 0)), _mod_spec(n_batch)],
        out_specs=pl.BlockSpec((TM, D), lambda i: (i, 0)),
        compiler_params=_cp(("parallel",), VMEM_BIG),
        name="merge",
    )(*outs, p, w_branch, w_out, h, mgate)


def _final_body(h_ref, g_ref, o_ref):
    o_ref[...] = _rms(h_ref[...]) * g_ref[...]


def _final_norm(h, gain, n_rows):
    return pl.pallas_call(
        _final_body,
        out_shape=jax.ShapeDtypeStruct((n_rows, D), f32),
        grid=(n_rows // TM,),
        in_specs=[pl.BlockSpec((TM, D), lambda i: (i, 0)), pl.BlockSpec((1, D), lambda i: (0, 0))],
        out_specs=pl.BlockSpec((TM, D), lambda i: (i, 0)),
        compiler_params=_cp(("parallel",)),
        name="final_norm",
    )(h, gain)


def _rope_table(half, n_heads):
    freqs = ROPE_THETA ** (-jnp.arange(half, dtype=f32) / half)
    pos = jnp.arange(S)
    ar = (pos // GRID_W).astype(f32)[:, None] * freqs
    ac = (pos % GRID_W).astype(f32)[:, None] * freqs
    cos_h = jnp.concatenate([jnp.cos(ar), jnp.cos(ar), jnp.cos(ac), jnp.cos(ac)], axis=-1)
    sin_h = jnp.concatenate([-jnp.sin(ar), jnp.sin(ar), -jnp.sin(ac), jnp.sin(ac)], axis=-1)
    return jnp.tile(cos_h, (1, n_heads)), jnp.tile(sin_h, (1, n_heads))


def _with_identity_rows(cos, sin, n):
    return (jnp.concatenate([cos, jnp.ones((n, cos.shape[1]), f32)], axis=0),
            jnp.concatenate([sin, jnp.zeros((n, sin.shape[1]), f32)], axis=0))


def _mla_rope_table():
    cos, sin = _rope_table(8, 1)
    pad = MLA_SLOT - MLA_NOPE - MLA_ROPE
    return (jnp.concatenate([jnp.ones((S, MLA_NOPE), f32), cos, jnp.ones((S, pad), f32)], axis=1),
            jnp.concatenate([jnp.zeros((S, MLA_NOPE), f32), sin, jnp.zeros((S, pad), f32)], axis=1))


def _pack_w_in(w_in):
    z = lambda n: jnp.zeros(w_in.shape[:-1] + (n,), w_in.dtype)
    sl = lambda a, b: w_in[..., a:b]
    return jnp.concatenate([
        sl(4512, 8608),
        sl(768, 2304),
        sl(0, 512),
        sl(2304, 2816) * (HEAD ** -0.5 * LOG2E),
        sl(2816, 3840),
        sl(4224, 4480),
        sl(512, 768),
        sl(3840, 4224), z(128),
        z(MLA_NOPE), sl(4480, 4512), z(MLA_SLOT - MLA_NOPE - MLA_ROPE),
    ], axis=-1).astype(bf16)


def _pack_wq(wq_b):
    w = wq_b.reshape(DEPTH, MLA_QR, MLA_H, MLA_NOPE + MLA_ROPE)
    w = jnp.pad(w, ((0, 0), (0, 512 - MLA_QR), (0, 0), (0, MLA_SLOT - MLA_NOPE - MLA_ROPE)))
    return w.reshape(DEPTH, 512, MLA_H * MLA_SLOT).astype(bf16)


def _pack_wkv(wkv_b):
    w = wkv_b.reshape(DEPTH, MLA_KVR, MLA_H, MLA_NOPE + MLA_V)
    slots = lambda a: jnp.pad(a, ((0, 0), (0, 0), (0, 0), (0, MLA_SLOT - a.shape[-1]))).reshape(DEPTH, MLA_KVR, -1)
    return jnp.concatenate([slots(w[..., :MLA_NOPE]), slots(w[..., MLA_NOPE:])], axis=-1).astype(bf16)


def kernel(x, c, ctx, c_ctx, ada_w, ada_b, ffn1_up, ffn1_down, w_in, gqa_q_norm, gqa_k_norm, hy_conv_w, hy_conv_b, hy_f_w1, hy_f_b, hy_f_freq, hy_f_w_mid, hy_f_w_out, hy_skip, na_rpb, mla_q_norm, mla_kv_norm, mla_wq_b, mla_wkv_b, w_branch, w_out, ffn2_up, ffn2_down, final_norm):
    nb = x.shape[0]
    rows_lat = nb * S
    rows_all = rows_lat + nb * NCTX

    cond = jnp.concatenate([c, c_ctx[None], jnp.zeros((16 - nb - 1, D), f32)], axis=0)
    mod = _adaln(cond, ada_w, ada_b)[:, :nb + 1].reshape(DEPTH, nb + 1, N_MOD, 1, D)
    mods = lambda l, k: mod[l, :, k]

    w1u, w1d = ffn1_up.astype(bf16), ffn1_down.astype(bf16)
    w2u, w2d = ffn2_up.astype(bf16), ffn2_down.astype(bf16)
    w_in_p = _pack_w_in(w_in)
    wq_p, wkv_p = _pack_wq(mla_wq_b), _pack_wkv(mla_wkv_b)
    wb, wo = w_branch.astype(bf16), w_out.astype(bf16)

    gq_tab = _with_identity_rows(*_rope_table(16, GQA_H), NCTX)
    gk_tab = _rope_table(16, GQA_KVH)
    mk_tab = _mla_rope_table()
    mq_tab = _with_identity_rows(jnp.tile(mk_tab[0], (1, MLA_H)), jnp.tile(mk_tab[1], (1, MLA_H)), NCTX)
    lane = jnp.arange(BW)
    gmat = (lane[:, None] // HEAD == lane[None, :] // HEAD).astype(bf16)
    na_bias = _na_bias(na_rpb)

    hy_w1 = jnp.pad(hy_f_w1, ((0, 0), (0, 64 - HY_EMB), (0, 0)))
    dft_lat, dft_ctx = _dft_tables(S), _dft_tables(NCTX)
    kp_lat, kq_lat = _hy_filter(S, _filter_features(S), hy_w1, hy_f_b, hy_f_freq, hy_f_w_mid, hy_f_w_out,
                                dft_lat[0], dft_lat[1])
    kp_ctx, kq_ctx = _hy_filter(NCTX, _filter_features(NCTX), hy_w1, hy_f_b, hy_f_freq, hy_f_w_mid, hy_f_w_out,
                                dft_ctx[0], dft_ctx[1])
    conv_w = hy_conv_w.reshape(DEPTH, 3, 3, HY_W).transpose(0, 2, 1, 3)
    conv_b = hy_conv_b.reshape(DEPTH, 3, 1, HY_W)

    h = jnp.concatenate([x.reshape(rows_lat, D), ctx.reshape(nb * NCTX, D)], axis=0)
    for l in range(DEPTH):
        with_ctx = l < DEPTH - 1
        n_out = rows_all if with_ctx else rows_lat
        h = _ffn(h, mods(l, 0), mods(l, 1), mods(l, 2), w1u[l], w1d[l], rows_all, nb)
        p = _proj(h, mods(l, 3), mods(l, 4), w_in_p[l], rows_all, nb)
        o_gqa = _gqa(p, gq_tab + gk_tab, jnp.tile(gqa_q_norm[l], GQA_H)[None], jnp.tile(gqa_k_norm[l], GQA_KVH)[None],
                     gmat, nb, with_ctx)
        o_hy = _hy_conv(p, S, 0, nb, conv_w[l], conv_b[l], hy_skip[l][None], dft_lat, kp_lat[l], kq_lat[l])
        if with_ctx:
            o_hy_c = _hy_conv(p, NCTX, rows_lat // NCTX, nb, conv_w[l], conv_b[l], hy_skip[l][None], dft_ctx,
                              kp_ctx[l], kq_ctx[l])
            o_hy = jnp.concatenate([o_hy, o_hy_c], axis=0)
        o_na = _na(p, na_bias[l], nb, with_ctx)
        o_mla = _mla(p, mq_tab + mk_tab, jnp.pad(mla_q_norm[l], (0, 512 - MLA_QR))[None], mla_kv_norm[l][None],
                     wq_p[l], wkv_p[l], nb, with_ctx)
        h = _merge((o_gqa, o_hy, o_na, o_mla), p, wb[l], wo[l], h, mods(l, 5), n_out, nb)
        h = _ffn(h, mods(l, 6), mods(l, 7), mods(l, 8), w2u[l], w2d[l], n_out, nb)
    out = _final_norm(h, final_norm[None], rows_lat)
    return out.reshape(nb, S, D)
```

```python
import functools
import math

import numpy as np
import jax
import jax.numpy as jnp
from jax import lax
from jax.experimental import pallas as pl
from jax.experimental.pallas import tpu as pltpu

f32 = jnp.float32
bf16 = jnp.bfloat16

D = 1024
S = 2048
DEPTH = 4
GRID_W = 64
NCTX = 256
HEAD = 64
ROPE_THETA = 10000.0
RMS_EPS = 1e-6
N_MOD = 9
F = 2816
BW = 512
GQA_H, GQA_KVH, GQA_GROUP = 8, 2, 4
NA_H, NA_ROWS, NA_COLS = 8, 8, 16
MLA_H, MLA_NOPE, MLA_ROPE, MLA_V = 8, 64, 32, 64
MLA_QR, MLA_KVR = 384, 256
MLA_SCALE = (MLA_NOPE + MLA_ROPE) ** -0.5
MLA_SLOT = 128
MLA_LOCKSTEP = 4
LOG2E = math.log2(math.e)
HY_W = 512
HY_BANDS = 16
HY_EMB = 1 + 2 * HY_BANDS
HY_HID = 64
HY_MIN_DECAY = math.log(1e-2) / 1.5
HY_MAX_DECAY = math.log(1e-2) / 0.3
NEG = -1e30

C_GATE = 0
C_HY = 4096
C_NK = 5632
C_MKVA = 6144
C_GK, C_GV = 6400, 6528
C_MKR = 6656
PW = 6784
R_NQ, R_NV, R_GQ, R_MQA = 0, 512, 1024, 1536
PT_ROWS = 1920

TQ = 256
TM = 512
VMEM_BIG = 56 * 1024 * 1024


def _cp(sem, vmem=None):
    return pltpu.CompilerParams(dimension_semantics=sem, vmem_limit_bytes=vmem)


def _resident(shape):
    nd = len(shape)
    return pl.BlockSpec(shape, lambda *_: (0,) * nd, pipeline_mode=pl.Buffered(1))


def _dot(a, b):
    return jnp.dot(a, b, preferred_element_type=f32)


def _dot_t(a, b):
    return lax.dot_general(a, b, (((1,), (1,)), ((), ())), preferred_element_type=f32)


def _rms(x):
    return x * lax.rsqrt(jnp.mean(x * x, axis=-1, keepdims=True) + RMS_EPS)


def _group_of_tile(i, n_batch):
    return jnp.minimum((i * TM) // S, n_batch)


def _mod_spec(n_batch):
    return pl.BlockSpec((None, 1, D), lambda i: (_group_of_tile(i, n_batch), 0, 0))


def _ada_body(c_ref, w_ref, b_ref, o_ref):
    x = c_ref[...]
    xs = (x * jax.nn.sigmoid(x)).astype(bf16)
    o_ref[...] = _dot(xs, w_ref[...].astype(bf16)) + b_ref[...]


def _adaln(cond, ada_w, ada_b):
    tn = 1024
    return pl.pallas_call(
        _ada_body,
        out_shape=jax.ShapeDtypeStruct((DEPTH, 16, N_MOD * D), f32),
        grid=(DEPTH, N_MOD * D // tn),
        in_specs=[pl.BlockSpec((16, D), lambda l, j: (0, 0)),
                  pl.BlockSpec((None, D, tn), lambda l, j: (l, 0, j)),
                  pl.BlockSpec((None, 1, tn), lambda l, j: (l, 0, j))],
        out_specs=pl.BlockSpec((None, 16, tn), lambda l, j: (l, 0, j)),
        compiler_params=_cp(("parallel", "parallel")),
        name="adaln",
    )(cond, ada_w, ada_b.reshape(DEPTH, 1, N_MOD * D))


def _ffn_body(h_ref, sh_ref, sc_ref, gt_ref, wup_ref, wdn_ref, o_ref):
    h = h_ref[...]
    xm = (_rms(h) * (1.0 + sc_ref[...]) + sh_ref[...]).astype(bf16)
    a = _dot(xm, wup_ref[:, :F])
    g = _dot(xm, wup_ref[:, F:])
    mid = (a * jax.nn.sigmoid(a) * g).astype(bf16)
    y = _dot(mid, wdn_ref[...])
    o_ref[...] = h + (0.5 * gt_ref[...]) * y


def _ffn(h, shift, scale, gate, w_up, w_down, n_rows, n_batch):
    tm = 256
    grp = lambda i: (jnp.minimum((i * tm) // S, n_batch), 0, 0)
    mspec = pl.BlockSpec((None, 1, D), grp)
    return pl.pallas_call(
        _ffn_body,
        out_shape=jax.ShapeDtypeStruct((n_rows, D), f32),
        grid=(n_rows // tm,),
        in_specs=[pl.BlockSpec((tm, D), lambda i: (i, 0)), mspec, mspec, mspec,
                  _resident((D, 2 * F)), _resident((F, D))],
        out_specs=pl.BlockSpec((tm, D), lambda i: (i, 0)),
        compiler_params=_cp(("parallel",), VMEM_BIG),
        name="ffn",
    )(h, shift, scale, gate, w_up, w_down)


PROJ_CHUNK = 2048


def _proj_body(h_ref, sh_ref, sc_ref, w_ref, wt_ref, o_ref, ot_ref):
    xm = (_rms(h_ref[...]) * (1.0 + sc_ref[...]) + sh_ref[...]).astype(bf16)
    for j in range(0, PW, PROJ_CHUNK):
        width = min(PROJ_CHUNK, PW - j)
        o_ref[:, j:j + width] = _dot(xm, w_ref[:, j:j + width]).astype(bf16)
    ot_ref[...] = _dot_t(wt_ref[...], xm).astype(bf16)


def _proj(h, shift, scale, w, w_t, n_rows, n_batch):
    return pl.pallas_call(
        _proj_body,
        out_shape=(jax.ShapeDtypeStruct((n_rows, PW), bf16), jax.ShapeDtypeStruct((PT_ROWS, n_rows), bf16)),
        grid=(n_rows // TM,),
        in_specs=[pl.BlockSpec((TM, D), lambda i: (i, 0)), _mod_spec(n_batch), _mod_spec(n_batch),
                  _resident((D, PW)), _resident((PT_ROWS, D))],
        out_specs=(pl.BlockSpec((TM, PW), lambda i: (i, 0)), pl.BlockSpec((PT_ROWS, TM), lambda i: (0, i))),
        compiler_params=_cp(("parallel",), VMEM_BIG),
        name="proj",
    )(h, shift, scale, w, w_t)


def _group_mean_sq(x, gmat, width):
    x2 = x * x
    hi = x2.astype(bf16)
    lo = (x2 - hi.astype(f32)).astype(bf16)
    return (_dot(hi, gmat) + _dot(lo, gmat)) * (1.0 / width)


def _swap_halves(x, half):
    n = x.shape[-1]
    lane = lax.broadcasted_iota(jnp.int32, x.shape, 1)
    first = (lane % (2 * half)) < half
    return jnp.where(first, pltpu.roll(x, n - half, 1), pltpu.roll(x, half, 1))


def _rope(x, cos, sin_signed, half):
    return x * cos + _swap_halves(x, half) * sin_signed


KCH = 256
VT_ROWS = 80


def _attend_key_major(get_k, get_vt, q_ts, n, dv, get_bias=None):
    nh, nc = len(q_ts), n // KCH
    tq = q_ts[0].shape[1]
    def scores(i, c):
        s = _dot(get_k(i, c), q_ts[i])
        return s if get_bias is None else s + get_bias(i, c)

    m = [jnp.full((1, tq), NEG, f32) for _ in range(nh)]
    o = [jnp.zeros((VT_ROWS, tq), f32) for _ in range(nh)]
    s_next = [scores(i, 0) for i in range(nh)]
    for c in range(nc):
        s_cur = s_next
        if c + 1 < nc:
            s_next = [scores(i, c + 1) for i in range(nh)]
        for i in range(nh):
            m_new = jnp.maximum(m[i], jnp.max(s_cur[i], axis=0, keepdims=True))
            e = jnp.exp2(s_cur[i] - m_new).astype(bf16)
            o[i] = o[i] * jnp.exp2(m[i] - m_new) + _dot(get_vt(i, c), e)
            m[i] = m_new
    return [oi[:dv, :] / oi[dv:dv + 1, :] for oi in o]


def _q_row_block(n_batch):
    nq = S // TQ
    return lambda b, qi: jnp.where(qi < nq, b * nq + qi, n_batch * nq + b)


def _swap_row_halves(x, half):
    parts = []
    for g in range(x.shape[0] // (2 * half)):
        parts += [x[g * 2 * half + half:(g + 1) * 2 * half, :], x[g * 2 * half:g * 2 * half + half, :]]
    return jnp.concatenate(parts, axis=0)


def _gqa_body(qt_ref, kl_ref, kc_ref, vl_ref, vc_ref, qcos_ref, qsin_ref, kcos_ref, ksin_ref,
              kg_ref, gm_ref, o_ref, k_s, vt_s, ot_s):
    qi = pl.program_id(1)
    nq = S // TQ

    @pl.when(qi == 0)
    def _():
        gmk = gm_ref[...]
        kl = kl_ref[...].astype(f32)
        kn = kl * lax.rsqrt(_group_mean_sq(kl, gmk, HEAD) + RMS_EPS) * kg_ref[...]
        kn = _rope(kn, kcos_ref[...], ksin_ref[...], 16).astype(bf16)
        kc = kc_ref[...].astype(f32)
        kcn = (kc * lax.rsqrt(_group_mean_sq(kc, gmk, HEAD) + RMS_EPS) * kg_ref[...]).astype(bf16)
        vl_t = vl_ref[...].astype(f32).T.astype(bf16)
        vc_t = vc_ref[...].astype(f32).T.astype(bf16)
        for g in range(GQA_KVH):
            k_s[g, :S, :] = kn[:, g * HEAD:(g + 1) * HEAD]
            k_s[g, S:, :] = kcn[:, g * HEAD:(g + 1) * HEAD]
            vt_s[g, :HEAD, :S] = vl_t[g * HEAD:(g + 1) * HEAD, :]
            vt_s[g, :HEAD, S:] = vc_t[g * HEAD:(g + 1) * HEAD, :]
            vt_s[g, HEAD:, :] = jnp.ones((VT_ROWS - HEAD, S + NCTX), bf16)

    q = qt_ref[...].astype(f32)
    q2 = q * q
    inv = [lax.rsqrt(jnp.sum(q2[h * HEAD:(h + 1) * HEAD, :], axis=0, keepdims=True) * (1.0 / HEAD) + RMS_EPS)
           for h in range(GQA_H)]
    qn = q * jnp.concatenate([jnp.broadcast_to(r, (HEAD, TQ)) for r in inv], axis=0)
    q_t = (qn * qcos_ref[...] + _swap_row_halves(qn, 16) * qsin_ref[...]).astype(bf16)

    def attend(lo):
        for g in range(GQA_KVH):
            heads = range(g * GQA_GROUP, (g + 1) * GQA_GROUP)
            outs = _attend_key_major(
                lambda i, c: k_s[g, lo + c * KCH:lo + (c + 1) * KCH, :],
                lambda i, c: vt_s[g, :, lo + c * KCH:lo + (c + 1) * KCH],
                [q_t[h * HEAD:(h + 1) * HEAD, :] for h in heads], S + NCTX - lo, HEAD)
            for h, o_t in zip(heads, outs):
                ot_s[h * HEAD:(h + 1) * HEAD, :] = o_t
        o_ref[...] = ot_s[...].T.astype(bf16)

    @pl.when(qi < nq)
    def _():
        attend(0)

    @pl.when(qi >= nq)
    def _():
        attend(S)


def _gqa(p, pt, tabs, k_gain, gmat, n_batch, with_ctx):
    nq = S // TQ
    n_rows = n_batch * S + (n_batch * NCTX if with_ctx else 0)
    qblk = _q_row_block(n_batch)
    ctx_blk = lambda b, qi: n_batch * nq + b
    qcos, qsin, kcos, ksin = tabs
    return pl.pallas_call(
        _gqa_body,
        out_shape=jax.ShapeDtypeStruct((n_rows, BW), bf16),
        grid=(n_batch, nq + (1 if with_ctx else 0)),
        in_specs=[
            pl.BlockSpec((BW, TQ), lambda b, qi: (R_GQ // BW, qblk(b, qi))),
            pl.BlockSpec((S, 128), lambda b, qi: (b, C_GK // 128)),
            pl.BlockSpec((NCTX, 128), lambda b, qi: (ctx_blk(b, qi), C_GK // 128)),
            pl.BlockSpec((S, 128), lambda b, qi: (b, C_GV // 128)),
            pl.BlockSpec((NCTX, 128), lambda b, qi: (ctx_blk(b, qi), C_GV // 128)),
            pl.BlockSpec((BW, TQ), lambda b, qi: (0, jnp.minimum(qi, nq))),
            pl.BlockSpec((BW, TQ), lambda b, qi: (0, jnp.minimum(qi, nq))),
            pl.BlockSpec((S, 128), lambda b, qi: (0, 0)),
            pl.BlockSpec((S, 128), lambda b, qi: (0, 0)),
            pl.BlockSpec((1, 128), lambda b, qi: (0, 0)),
            pl.BlockSpec((2 * HEAD, 2 * HEAD), lambda b, qi: (0, 0)),
        ],
        out_specs=pl.BlockSpec((TQ, BW), lambda b, qi: (qblk(b, qi), 0)),
        scratch_shapes=[pltpu.VMEM((GQA_KVH, S + NCTX, HEAD), bf16),
                        pltpu.VMEM((GQA_KVH, VT_ROWS, S + NCTX), bf16),
                        pltpu.VMEM((BW, TQ), f32)],
        compiler_params=_cp(("parallel", "arbitrary"), VMEM_BIG),
        name="gqa",
    )(pt, p, p, p, p, qcos, qsin, kcos, ksin, k_gain, gmat)


def _mla_body(qat_ref, kval_ref, kvac_ref, krl_ref, krc_ref, qcos_ref, qsin_ref, kcos_ref, ksin_ref,
              qg_ref, kvg_ref, wqt_ref, wkv_ref, o_ref, k_s, vt_s, ot_s):
    qi = pl.program_id(1)
    nq = S // TQ
    slot = MLA_SLOT

    @pl.when(qi == 0)
    def _():
        lane = lax.broadcasted_iota(jnp.int32, (1, 2 * slot), 1)
        ones_hi = jnp.where(lane % slot >= MLA_V, 1.0, 0.0)

        def kv(ref, kr, lo, n):
            a = (_rms(ref[...].astype(f32)) * kvg_ref[...]).astype(bf16)
            for hp in range(MLA_H // 2):
                kk = _dot(a, wkv_ref[:, hp * 2 * slot:(hp + 1) * 2 * slot])
                vv = _dot(a, wkv_ref[:, (MLA_H + hp * 2) * slot:(MLA_H + hp * 2 + 2) * slot])
                vv_t = (vv + ones_hi).T.astype(bf16)
                for i in range(2):
                    k_s[2 * hp + i, lo:lo + n, :] = (kk[:, i * slot:(i + 1) * slot] + kr).astype(bf16)
                    vt_s[2 * hp + i, :, lo:lo + n] = vv_t[i * slot:i * slot + VT_ROWS, :]

        kv(kval_ref, _rope(krl_ref[...].astype(f32), kcos_ref[...], ksin_ref[...], 8), 0, S)
        kv(kvac_ref, krc_ref[...].astype(f32), S, NCTX)

    qa = qat_ref[...].astype(f32)
    qan = qa * lax.rsqrt(jnp.mean(qa * qa, axis=0, keepdims=True) + RMS_EPS)
    qh = _dot(wqt_ref[...], (qan * qg_ref[...]).astype(bf16))
    q_t = (qh * qcos_ref[...] + _swap_row_halves(qh, 8) * qsin_ref[...]).astype(bf16)

    def attend(lo):
        for h0 in range(0, MLA_H, MLA_LOCKSTEP):
            heads = range(h0, h0 + MLA_LOCKSTEP)
            outs = _attend_key_major(
                lambda i, c: k_s[h0 + i, lo + c * KCH:lo + (c + 1) * KCH, :],
                lambda i, c: vt_s[h0 + i, :, lo + c * KCH:lo + (c + 1) * KCH],
                [q_t[h * slot:(h + 1) * slot, :] for h in heads], S + NCTX - lo, MLA_V)
            for h, o_t in zip(heads, outs):
                ot_s[h * MLA_V:(h + 1) * MLA_V, :] = o_t
        o_ref[...] = ot_s[...].T.astype(bf16)

    @pl.when(qi < nq)
    def _():
        attend(0)

    @pl.when(qi >= nq)
    def _():
        attend(S)


def _mla(p, pt, tabs, q_gain, kv_gain, wq_t, wkv, n_batch, with_ctx):
    nq = S // TQ
    n_rows = n_batch * S + (n_batch * NCTX if with_ctx else 0)
    qblk = _q_row_block(n_batch)
    ctx_blk = lambda b, qi: n_batch * nq + b
    qcos, qsin, kcos, ksin = tabs
    rw = MLA_H * MLA_SLOT
    return pl.pallas_call(
        _mla_body,
        out_shape=jax.ShapeDtypeStruct((n_rows, BW), bf16),
        grid=(n_batch, nq + (1 if with_ctx else 0)),
        in_specs=[
            pl.BlockSpec((MLA_QR, TQ), lambda b, qi: (R_MQA // MLA_QR, qblk(b, qi))),
            pl.BlockSpec((S, MLA_KVR), lambda b, qi: (b, C_MKVA // MLA_KVR)),
            pl.BlockSpec((NCTX, MLA_KVR), lambda b, qi: (ctx_blk(b, qi), C_MKVA // MLA_KVR)),
            pl.BlockSpec((S, 128), lambda b, qi: (b, C_MKR // 128)),
            pl.BlockSpec((NCTX, 128), lambda b, qi: (ctx_blk(b, qi), C_MKR // 128)),
            pl.BlockSpec((rw, TQ), lambda b, qi: (0, jnp.minimum(qi, nq))),
            pl.BlockSpec((rw, TQ), lambda b, qi: (0, jnp.minimum(qi, nq))),
            pl.BlockSpec((S, 128), lambda b, qi: (0, 0)),
            pl.BlockSpec((S, 128), lambda b, qi: (0, 0)),
            pl.BlockSpec((MLA_QR, TQ), lambda b, qi: (0, 0)),
            pl.BlockSpec((1, MLA_KVR), lambda b, qi: (0, 0)),
            pl.BlockSpec((MLA_H * MLA_SLOT, MLA_QR), lambda b, qi: (0, 0)),
            pl.BlockSpec((MLA_KVR, 2 * MLA_H * MLA_SLOT), lambda b, qi: (0, 0)),
        ],
        out_specs=pl.BlockSpec((TQ, BW), lambda b, qi: (qblk(b, qi), 0)),
        scratch_shapes=[pltpu.VMEM((MLA_H, S + NCTX, MLA_SLOT), bf16),
                        pltpu.VMEM((MLA_H, VT_ROWS, S + NCTX), bf16),
                        pltpu.VMEM((BW, TQ), f32)],
        compiler_params=_cp(("parallel", "arbitrary"), VMEM_BIG),
        name="mla",
    )(pt, p, p, p, p, qcos, qsin, kcos, ksin, q_gain, kv_gain, wq_t, wkv)


NA_LOCKSTEP = 4


def _na_body(qt_ref, k0_ref, k1_ref, k2_ref, kc_ref, vt0_ref, vt1_ref, vt2_ref, vtc_ref, bias_ref, o_ref, ot_s):
    t = pl.program_id(0)
    nt = S // TQ
    row = lax.broadcasted_iota(jnp.int32, (2 * HEAD, TQ), 0)
    ones = jnp.ones((VT_ROWS - HEAD, KCH), bf16)

    def run(k_refs, vt_refs, with_bias):
        for h0 in range(0, NA_H, NA_LOCKSTEP):
            q_ts = []
            for pair in range(h0 // 2, (h0 + NA_LOCKSTEP) // 2):
                qp = qt_ref[pair * 2 * HEAD:(pair + 1) * 2 * HEAD, :]
                q_ts += [jnp.where(row < HEAD, qp, jnp.zeros_like(qp)), jnp.where(row >= HEAD, qp, jnp.zeros_like(qp))]
            outs = _attend_key_major(
                lambda i, c: k_refs[c][:, ((h0 + i) // 2) * 2 * HEAD:((h0 + i) // 2 + 1) * 2 * HEAD],
                lambda i, c: jnp.concatenate([vt_refs[c][(h0 + i) * HEAD:(h0 + i + 1) * HEAD, :], ones], axis=0),
                q_ts, len(k_refs) * KCH, HEAD,
                (lambda i, c: bias_ref[h0 + i, c * KCH:(c + 1) * KCH, :]) if with_bias else None)
            for i, o_t in enumerate(outs):
                ot_s[(h0 + i) * HEAD:(h0 + i + 1) * HEAD, :] = o_t
        o_ref[...] = ot_s[...].T.astype(bf16)

    @pl.when(t < nt)
    def _():
        run([k0_ref, k1_ref, k2_ref, kc_ref], [vt0_ref, vt1_ref, vt2_ref, vtc_ref], True)

    @pl.when(t >= nt)
    def _():
        run([kc_ref], [vtc_ref], False)


def _na(p, pt, bias, n_batch, with_ctx):
    nt = S // TQ
    n_rows = n_batch * S + (n_batch * NCTX if with_ctx else 0)
    qblk = lambda t, b: jnp.where(t < nt, b * nt + t, n_batch * nt + b)
    ctx_blk = lambda t, b: n_batch * nt + b
    centre = lambda t: jnp.clip(t, 1, nt - 2)
    variant = lambda t: jnp.where(t == 0, 0, jnp.where(t >= nt - 1, 2, 1))

    def kwin(d):
        return pl.BlockSpec((TQ, BW), lambda t, b: (b * nt + centre(t) + d, C_NK // BW))

    def vwin(d):
        return pl.BlockSpec((BW, TQ), lambda t, b: (R_NV // BW, b * nt + centre(t) + d))

    return pl.pallas_call(
        _na_body,
        out_shape=jax.ShapeDtypeStruct((n_rows, BW), bf16),
        grid=(nt + (1 if with_ctx else 0), n_batch),
        in_specs=[
            pl.BlockSpec((BW, TQ), lambda t, b: (R_NQ // BW, qblk(t, b))),
            kwin(-1), kwin(0), kwin(1),
            pl.BlockSpec((NCTX, BW), lambda t, b: (ctx_blk(t, b), C_NK // BW)),
            vwin(-1), vwin(0), vwin(1),
            pl.BlockSpec((BW, NCTX), lambda t, b: (R_NV // BW, ctx_blk(t, b))),
            pl.BlockSpec((None, NA_H, 4 * TQ, TQ), lambda t, b: (variant(t), 0, 0, 0)),
        ],
        out_specs=pl.BlockSpec((TQ, BW), lambda t, b: (qblk(t, b), 0)),
        scratch_shapes=[pltpu.VMEM((BW, TQ), f32)],
        compiler_params=_cp(("arbitrary", "arbitrary"), VMEM_BIG),
        name="na",
    )(pt, p, p, p, p, pt, pt, pt, pt, bias)


def _na_bias(rpb):
    rows_per_tile = TQ // GRID_W
    n_rows = S // GRID_W
    kc = np.arange(GRID_W)[:, None]
    qc = np.arange(GRID_W)[None, :]
    col_start = np.clip(qc - NA_COLS // 2, 0, GRID_W - NA_COLS)
    col_ok = (kc >= col_start) & (kc < col_start + NA_COLS)
    cidx = np.clip(kc - qc + NA_COLS - 1, 0, 2 * NA_COLS - 2)
    onehot = (np.arange(2 * NA_COLS - 1)[:, None, None] == cidx[None]).astype(np.float32)
    blocks = jnp.einsum('lhrc,ckq->lhrkq', rpb.astype(f32), jnp.asarray(onehot), precision=lax.Precision.HIGHEST)
    blocks = jnp.where(jnp.asarray(col_ok), blocks * LOG2E, NEG)
    masked = jnp.full((DEPTH, NA_H, GRID_W, GRID_W), NEG, f32)
    out = []
    for t in (0, 1, S // TQ - 1):
        centre = min(max(t, 1), S // TQ - 2)
        strips = []
        for qoff in range(rows_per_tile):
            r = rows_per_tile * t + qoff
            start = min(max(r - NA_ROWS // 2, 0), n_rows - NA_ROWS)
            strip = []
            for koff in range(3 * rows_per_tile):
                krow = rows_per_tile * (centre - 1) + koff
                in_band = start <= krow < start + NA_ROWS
                strip.append(blocks[:, :, krow - r + NA_ROWS - 1] if in_band else masked)
            strip.append(jnp.zeros((DEPTH, NA_H, NCTX, GRID_W), f32))
            strips.append(jnp.concatenate(strip, axis=-2))
        out.append(jnp.concatenate(strips, axis=-1))
    return jnp.stack(out, axis=1)


def _dft_tables(length):
    n = 2 * length
    f = jnp.arange(length, dtype=jnp.int32)[:, None]
    s = jnp.arange(length, dtype=jnp.int32)[None, :]
    ang = ((f * s) % n).astype(f32) * (2.0 * math.pi / n)
    return jnp.cos(ang).astype(bf16), jnp.sin(ang).astype(bf16)


def _filter_features(length):
    t = jnp.linspace(0.0, 1.0, length, dtype=f32)[:, None]
    w = (2.0 * math.pi / length) * jnp.arange(length, dtype=f32)[:, None]
    fb = jnp.linspace(1e-4, HY_BANDS - 1, HY_BANDS, dtype=f32)[None, :]
    z = jnp.concatenate([t, jnp.cos(fb * w), -jnp.sin(fb * w)], axis=-1)
    deltas = jnp.abs(jnp.linspace(HY_MIN_DECAY, HY_MAX_DECAY, HY_W, dtype=f32))
    dec = jnp.exp(-t * deltas)
    z = jnp.pad(z, ((0, 0), (0, 64 - HY_EMB)))
    z_rev = jnp.concatenate([z[:1], z[:0:-1]], axis=0)
    dec_rev = jnp.concatenate([jnp.zeros((1, HY_W), f32), dec[:0:-1]], axis=0)
    return z, z_rev, dec, dec_rev


HY_CW = 256
HY_FC = 512


def _alternating(shape):
    row = lax.broadcasted_iota(jnp.int32, shape, 0)
    return jnp.where(row % 2 == 0, 1.0, -1.0)


def _hyfilt_body(z_ref, zr_ref, dec_ref, decr_ref, w1_ref, b_ref, fr_ref, wm_ref, wof_ref, wob_ref, a_ref, sn_ref,
                 kp_ref, kq_ref, hf_s, hb_s):
    length = z_ref.shape[0]
    fc = min(HY_FC, length)

    def hidden(z):
        hdn = jnp.sin(fr_ref[0:1, :] * (_dot(z, w1_ref[...]) + b_ref[0:1, :]))
        hdn = jnp.sin(fr_ref[1:2, :] * (_dot(hdn, wm_ref[0]) + b_ref[1:2, :]))
        return jnp.sin(fr_ref[2:3, :] * (_dot(hdn, wm_ref[1]) + b_ref[2:3, :]))

    @pl.when(pl.program_id(1) == 0)
    def _():
        hf_s[...] = hidden(z_ref[...])
        hb_s[...] = hidden(zr_ref[...])

    ff = _dot(hf_s[...], wof_ref[...]) * dec_ref[...]
    fb = _dot(hb_s[...], wob_ref[...]) * decr_ref[...]
    norm = jnp.sum(jnp.abs(ff), axis=0, keepdims=True) + jnp.sum(jnp.abs(fb), axis=0, keepdims=True)
    ff = ff / norm
    fb = fb / norm
    inv_n = 1.0 / (2 * length)
    nyq = jnp.sum(_alternating((length, HY_CW)) * (ff + fb), axis=0, keepdims=True) * inv_n
    ff_b = ff.astype(bf16)
    fb_b = fb.astype(bf16)
    sign = _alternating((fc, HY_CW))
    for c in range(length // fc):
        rows = slice(c * fc, (c + 1) * fc)
        a = a_ref[rows, :]
        sn = sn_ref[rows, :]
        row = lax.broadcasted_iota(jnp.int32, (fc, HY_CW), 0) + c * fc
        wgt = jnp.where(row == 0, inv_n, 2.0 * inv_n)
        kp_ref[rows, :] = (_dot(a, ff_b) + sign * _dot(a, fb_b)) * wgt
        kq = (_dot(sn, ff_b) + sign * _dot(sn, fb_b)) * wgt
        kq_ref[rows, :] = jnp.where(row == 0, nyq, kq)


def _hy_filter(length, feats, w1, fb, ffreq, wmid, wout, a, sn):
    z, z_rev, dec, dec_rev = feats
    nj = HY_W // HY_CW
    full = lambda shape: pl.BlockSpec(shape, lambda l, j: (0,) * len(shape))
    chan = lambda shape: pl.BlockSpec(shape, lambda l, j: (0, j))
    per_layer = lambda shape: pl.BlockSpec((None,) + shape, lambda l, j: (l,) + (0,) * len(shape))
    out_spec = pl.BlockSpec((None, length, HY_CW), lambda l, j: (l, 0, j))
    return pl.pallas_call(
        _hyfilt_body,
        out_shape=(jax.ShapeDtypeStruct((DEPTH, length, HY_W), f32),
                   jax.ShapeDtypeStruct((DEPTH, length, HY_W), f32)),
        grid=(DEPTH, nj),
        in_specs=[full((length, 64)), full((length, 64)), chan((length, HY_CW)), chan((length, HY_CW)),
                  per_layer((64, HY_HID)), per_layer((3, HY_HID)), per_layer((3, HY_HID)),
                  per_layer((2, HY_HID, HY_HID)),
                  pl.BlockSpec((None, HY_HID, HY_CW), lambda l, j: (l, 0, j)),
                  pl.BlockSpec((None, HY_HID, HY_CW), lambda l, j: (l, 0, nj + j)),
                  full((length, length)), full((length, length))],
        out_specs=(out_spec, out_spec),
        scratch_shapes=[pltpu.VMEM((length, HY_HID), f32), pltpu.VMEM((length, HY_HID), f32)],
        compiler_params=_cp(("parallel", "arbitrary"), VMEM_BIG),
        name="hyena_filter",
    )(z, z_rev, dec, dec_rev, w1, fb, ffreq, wmid, wout, wout, a, sn)


def _hyconv_body(x0_ref, x1_ref, v_ref, cw_ref, cb_ref, skip_ref, a_ref, sn_ref, kp_ref, kq_ref, o_ref):
    length = x0_ref.shape[0]
    fc = min(HY_FC, length)
    row = lax.broadcasted_iota(jnp.int32, (length, HY_CW), 0)

    def short_conv(u_ref, part):
        u = u_ref[...].astype(f32)
        prev = jnp.where(row == 0, 0.0, pltpu.roll(u, 1, 0))
        nxt = jnp.where(row == length - 1, 0.0, pltpu.roll(u, length - 1, 0))
        w = cw_ref[part]
        return prev * w[0:1, :] + u * w[1:2, :] + nxt * w[2:3, :] + cb_ref[part]

    x0 = short_conv(x0_ref, 0)
    x1 = short_conv(x1_ref, 1)
    vv = short_conv(v_ref, 2) * x1
    vb = vv.astype(bf16)
    alt = _alternating((length, HY_CW))
    y = alt * (jnp.sum(alt * vv, axis=0, keepdims=True) * kq_ref[0:1, :])
    for c in range(length // fc):
        rows = slice(c * fc, (c + 1) * fc)
        pc = _dot(a_ref[rows, :], vb)
        qs = _dot(sn_ref[rows, :], vb)
        kp = kp_ref[rows, :]
        kq = kq_ref[rows, :]
        yp = (pc * kp - qs * kq).astype(bf16)
        yq = (pc * kq + qs * kp).astype(bf16)
        y = y + _dot(a_ref[:, rows], yp) + _dot(sn_ref[:, rows], yq)
    o_ref[...] = ((y + vv * skip_ref[...]) * x0).astype(bf16)


def _hy_conv(p, length, row_blk0, n_seq, conv_w, conv_b, skip, tabs, kp, kq):
    a, sn = tabs
    nj = HY_W // HY_CW
    c0 = C_HY // HY_CW

    def part(k):
        return pl.BlockSpec((length, HY_CW), lambda j, b: (row_blk0 + b, c0 + k * nj + j))

    chan3 = lambda shape: pl.BlockSpec(shape, lambda j, b: (0, 0, j))
    return pl.pallas_call(
        _hyconv_body,
        out_shape=jax.ShapeDtypeStruct((n_seq * length, HY_W), bf16),
        grid=(nj, n_seq),
        in_specs=[part(0), part(1), part(2),
                  chan3((3, 3, HY_CW)), chan3((3, 1, HY_CW)),
                  pl.BlockSpec((1, HY_CW), lambda j, b: (0, j)),
                  _resident((length, length)), _resident((length, length)),
                  pl.BlockSpec((length, HY_CW), lambda j, b: (0, j)),
                  pl.BlockSpec((length, HY_CW), lambda j, b: (0, j))],
        out_specs=pl.BlockSpec((length, HY_CW), lambda j, b: (b, j)),
        compiler_params=_cp(("parallel", "parallel"), VMEM_BIG),
        name="hyena_conv",
    )(p, p, p, conv_w, conv_b, skip, a, sn, kp, kq)


def _merge_body(og_ref, oh_ref, on_ref, om_ref, gate_ref, wb_ref, wo_ref, h_ref, m_ref, o_ref):
    merged = None
    for i, o in enumerate((og_ref, oh_ref, on_ref, om_ref)):
        g = jax.nn.sigmoid(gate_ref[:, i * D:(i + 1) * D].astype(f32))
        term = g * _dot(o[...], wb_ref[i])
        merged = term if merged is None else merged + term
    mix = _dot(merged.astype(bf16), wo_ref[...])
    o_ref[...] = h_ref[...] + m_ref[...] * mix


def _merge(outs, p, w_branch, w_out, h, mgate, n_rows, n_batch):
    ospec = pl.BlockSpec((TM, BW), lambda i: (i, 0))
    return pl.pallas_call(
        _merge_body,
        out_shape=jax.ShapeDtypeStruct((n_rows, D), f32),
        grid=(n_rows // TM,),
        in_specs=[ospec, ospec, ospec, ospec,
                  pl.BlockSpec((TM, 4 * D), lambda i: (i, C_GATE // (4 * D))),
                  _resident((4, BW, D)), _resident((D, D)),
                  pl.BlockSpec((TM, D), lambda i: (i, 0)), _mod_spec(n_batch)],
        out_specs=pl.BlockSpec((TM, D), lambda i: (i, 0)),
        compiler_params=_cp(("parallel",), VMEM_BIG),
        name="merge",
    )(*outs, p, w_branch, w_out, h, mgate)


def _final_body(h_ref, g_ref, o_ref):
    o_ref[...] = _rms(h_ref[...]) * g_ref[...]


def _final_norm(h, gain, n_rows):
    return pl.pallas_call(
        _final_body,
        out_shape=jax.ShapeDtypeStruct((n_rows, D), f32),
        grid=(n_rows // TM,),
        in_specs=[pl.BlockSpec((TM, D), lambda i: (i, 0)), pl.BlockSpec((1, D), lambda i: (0, 0))],
        out_specs=pl.BlockSpec((TM, D), lambda i: (i, 0)),
        compiler_params=_cp(("parallel",)),
        name="final_norm",
    )(h, gain)


def _rope_table(half, n_heads):
    freqs = ROPE_THETA ** (-jnp.arange(half, dtype=f32) / half)
    pos = jnp.arange(S)
    ar = (pos // GRID_W).astype(f32)[:, None] * freqs
    ac = (pos % GRID_W).astype(f32)[:, None] * freqs
    cos_h = jnp.concatenate([jnp.cos(ar), jnp.cos(ar), jnp.cos(ac), jnp.cos(ac)], axis=-1)
    sin_h = jnp.concatenate([-jnp.sin(ar), jnp.sin(ar), -jnp.sin(ac), jnp.sin(ac)], axis=-1)
    return jnp.tile(cos_h, (1, n_heads)), jnp.tile(sin_h, (1, n_heads))


def _with_identity_rows(cos, sin, n):
    return (jnp.concatenate([cos, jnp.ones((n, cos.shape[1]), f32)], axis=0),
            jnp.concatenate([sin, jnp.zeros((n, sin.shape[1]), f32)], axis=0))


def _gqa_q_tables(tab, q_gain):
    cos, sin = tab
    g = jnp.tile(q_gain, GQA_H)
    g_partner = g.reshape(-1, 2, 16)[:, ::-1, :].reshape(-1)
    scale = HEAD ** -0.5 * LOG2E
    return (cos.T * (g * scale)[:, None], sin.T * (g_partner * scale)[:, None])


def _mla_rope_table():
    cos, sin = _rope_table(8, 1)
    pad = MLA_SLOT - MLA_NOPE - MLA_ROPE
    return (jnp.concatenate([jnp.ones((S, MLA_NOPE), f32), cos, jnp.ones((S, pad), f32)], axis=1),
            jnp.concatenate([jnp.zeros((S, MLA_NOPE), f32), sin, jnp.zeros((S, pad), f32)], axis=1))


def _pack_w_in(w_in):
    z = lambda n: jnp.zeros(w_in.shape[:-1] + (n,), w_in.dtype)
    sl = lambda a, b: w_in[..., a:b]
    return jnp.concatenate([
        sl(4512, 8608),
        sl(768, 2304),
        sl(2816, 3328),
        sl(4224, 4480),
        sl(512, 768),
        z(MLA_NOPE), sl(4480, 4512), z(MLA_SLOT - MLA_NOPE - MLA_ROPE),
    ], axis=-1).astype(bf16)


def _pack_w_in_t(w_in):
    sl = lambda a, b: jnp.swapaxes(w_in[..., a:b], -1, -2)
    return jnp.concatenate([
        sl(2304, 2816) * (HEAD ** -0.5 * LOG2E),
        sl(3328, 3840),
        sl(0, 512),
        sl(3840, 4224),
    ], axis=-2).astype(bf16)


def _pack_wq(wq_b):
    w = wq_b.reshape(DEPTH, MLA_QR, MLA_H, MLA_NOPE + MLA_ROPE)
    w = jnp.pad(w, ((0, 0), (0, 0), (0, 0), (0, MLA_SLOT - MLA_NOPE - MLA_ROPE)))
    return jnp.swapaxes(w.reshape(DEPTH, MLA_QR, MLA_H * MLA_SLOT), 1, 2).astype(bf16)


def _pack_wkv(wkv_b):
    w = wkv_b.reshape(DEPTH, MLA_KVR, MLA_H, MLA_NOPE + MLA_V)
    slots = lambda a: jnp.pad(a, ((0, 0), (0, 0), (0, 0), (0, MLA_SLOT - a.shape[-1]))).reshape(DEPTH, MLA_KVR, -1)
    return jnp.concatenate([slots(w[..., :MLA_NOPE]), slots(w[..., MLA_NOPE:])], axis=-1).astype(bf16)


def kernel(x, c, ctx, c_ctx, ada_w, ada_b, ffn1_up, ffn1_down, w_in, gqa_q_norm, gqa_k_norm, hy_conv_w, hy_conv_b, hy_f_w1, hy_f_b, hy_f_freq, hy_f_w_mid, hy_f_w_out, hy_skip, na_rpb, mla_q_norm, mla_kv_norm, mla_wq_b, mla_wkv_b, w_branch, w_out, ffn2_up, ffn2_down, final_norm):
    nb = x.shape[0]
    rows_lat = nb * S
    rows_all = rows_lat + nb * NCTX

    cond = jnp.concatenate([c, c_ctx[None], jnp.zeros((16 - nb - 1, D), f32)], axis=0)
    mod = _adaln(cond, ada_w, ada_b)[:, :nb + 1].reshape(DEPTH, nb + 1, N_MOD, 1, D)
    mods = lambda l, k: mod[l, :, k]

    w1u, w1d = ffn1_up.astype(bf16), ffn1_down.astype(bf16)
    w2u, w2d = ffn2_up.astype(bf16), ffn2_down.astype(bf16)
    w_in_p, w_in_t = _pack_w_in(w_in), _pack_w_in_t(w_in)
    wq_p, wkv_p = _pack_wq(mla_wq_b), _pack_wkv(mla_wkv_b)
    wb, wo = w_branch.astype(bf16), w_out.astype(bf16)

    gq_tab = _with_identity_rows(*_rope_table(16, GQA_H), NCTX)
    gk_tab = _rope_table(16, GQA_KVH)
    mk_tab = _mla_rope_table()
    mq_tab = _with_identity_rows(jnp.tile(mk_tab[0], (1, MLA_H)), jnp.tile(mk_tab[1], (1, MLA_H)), NCTX)
    mq_tab = tuple(t.T * (MLA_SCALE * LOG2E) for t in mq_tab)
    lane = jnp.arange(GQA_KVH * HEAD)
    gmat = (lane[:, None] // HEAD == lane[None, :] // HEAD).astype(bf16)
    na_bias = _na_bias(na_rpb)

    hy_w1 = jnp.pad(hy_f_w1, ((0, 0), (0, 64 - HY_EMB), (0, 0)))
    dft_lat, dft_ctx = _dft_tables(S), _dft_tables(NCTX)
    kp_lat, kq_lat = _hy_filter(S, _filter_features(S), hy_w1, hy_f_b, hy_f_freq, hy_f_w_mid, hy_f_w_out,
                                dft_lat[0], dft_lat[1])
    kp_ctx, kq_ctx = _hy_filter(NCTX, _filter_features(NCTX), hy_w1, hy_f_b, hy_f_freq, hy_f_w_mid, hy_f_w_out,
                                dft_ctx[0], dft_ctx[1])
    conv_w = hy_conv_w.reshape(DEPTH, 3, 3, HY_W).transpose(0, 2, 1, 3)
    conv_b = hy_conv_b.reshape(DEPTH, 3, 1, HY_W)

    h = jnp.concatenate([x.reshape(rows_lat, D), ctx.reshape(nb * NCTX, D)], axis=0)
    for l in range(DEPTH):
        with_ctx = l < DEPTH - 1
        n_out = rows_all if with_ctx else rows_lat
        h = _ffn(h, mods(l, 0), mods(l, 1), mods(l, 2), w1u[l], w1d[l], rows_all, nb)
        p, pt = _proj(h, mods(l, 3), mods(l, 4), w_in_p[l], w_in_t[l], rows_all, nb)
        o_gqa = _gqa(p, pt, _gqa_q_tables(gq_tab, gqa_q_norm[l]) + gk_tab, jnp.tile(gqa_k_norm[l], GQA_KVH)[None],
                     gmat, nb, with_ctx)
        o_hy = _hy_conv(p, S, 0, nb, conv_w[l], conv_b[l], hy_skip[l][None], dft_lat, kp_lat[l], kq_lat[l])
        if with_ctx:
            o_hy_c = _hy_conv(p, NCTX, rows_lat // NCTX, nb, conv_w[l], conv_b[l], hy_skip[l][None], dft_ctx,
                              kp_ctx[l], kq_ctx[l])
            o_hy = jnp.concatenate([o_hy, o_hy_c], axis=0)
        o_na = _na(p, pt, na_bias[l], nb, with_ctx)
        o_mla = _mla(p, pt, mq_tab + mk_tab, jnp.broadcast_to(mla_q_norm[l][:, None], (MLA_QR, TQ)),
                     mla_kv_norm[l][None], wq_p[l], wkv_p[l], nb, with_ctx)
        h = _merge((o_gqa, o_hy, o_na, o_mla), p, wb[l], wo[l], h, mods(l, 5), n_out, nb)
        h = _ffn(h, mods(l, 6), mods(l, 7), mods(l, 8), w2u[l], w2d[l], n_out, nb)
    out = _final_norm(h, final_norm[None], rows_lat)
    return out.reshape(nb, S, D)
```

```python
import functools
import math

import numpy as np
import jax
import jax.numpy as jnp
from jax import lax
from jax.experimental import pallas as pl
from jax.experimental.pallas import tpu as pltpu

f32 = jnp.float32
bf16 = jnp.bfloat16

D = 1024
S = 2048
DEPTH = 4
GRID_W = 64
NCTX = 256
HEAD = 64
ROPE_THETA = 10000.0
RMS_EPS = 1e-6
N_MOD = 9
F = 2816
BW = 512
GQA_H, GQA_KVH, GQA_GROUP = 8, 2, 4
NA_H, NA_ROWS, NA_COLS = 8, 8, 16
MLA_H, MLA_NOPE, MLA_ROPE, MLA_V = 8, 64, 32, 64
MLA_QR, MLA_KVR = 384, 256
MLA_SCALE = (MLA_NOPE + MLA_ROPE) ** -0.5
MLA_SLOT = 128
MLA_LOCKSTEP = 4
LOG2E = math.log2(math.e)
HY_W = 512
HY_BANDS = 16
HY_EMB = 1 + 2 * HY_BANDS
HY_HID = 64
HY_MIN_DECAY = math.log(1e-2) / 1.5
HY_MAX_DECAY = math.log(1e-2) / 0.3
NEG = -1e30

C_GATE = 0
C_HY = 4096
C_NK = 5632
C_MKVA = 6144
C_GK, C_GV = 6400, 6528
C_MKR = 6656
PW = 6784
R_NQ, R_NV, R_GQ, R_MQA = 0, 512, 1024, 1536
PT_ROWS = 1920

TQ = 256
TM = 512
VMEM_BIG = 56 * 1024 * 1024


def _cp(sem, vmem=None):
    return pltpu.CompilerParams(dimension_semantics=sem, vmem_limit_bytes=vmem)


def _resident(shape):
    nd = len(shape)
    return pl.BlockSpec(shape, lambda *_: (0,) * nd, pipeline_mode=pl.Buffered(1))


def _dot(a, b):
    return jnp.dot(a, b, preferred_element_type=f32)


def _dot_t(a, b):
    return lax.dot_general(a, b, (((1,), (1,)), ((), ())), preferred_element_type=f32)


def _rms(x):
    return x * lax.rsqrt(jnp.mean(x * x, axis=-1, keepdims=True) + RMS_EPS)


def _group_of_tile(i, n_batch):
    return jnp.minimum((i * TM) // S, n_batch)


def _mod_spec(n_batch):
    return pl.BlockSpec((None, 1, D), lambda i: (_group_of_tile(i, n_batch), 0, 0))


def _ada_body(c_ref, w_ref, b_ref, o_ref):
    x = c_ref[...]
    xs = (x * jax.nn.sigmoid(x)).astype(bf16)
    o_ref[...] = _dot(xs, w_ref[...].astype(bf16)) + b_ref[...]


def _adaln(cond, ada_w, ada_b):
    tn = 1024
    return pl.pallas_call(
        _ada_body,
        out_shape=jax.ShapeDtypeStruct((DEPTH, 16, N_MOD * D), f32),
        grid=(DEPTH, N_MOD * D // tn),
        in_specs=[pl.BlockSpec((16, D), lambda l, j: (0, 0)),
                  pl.BlockSpec((None, D, tn), lambda l, j: (l, 0, j)),
                  pl.BlockSpec((None, 1, tn), lambda l, j: (l, 0, j))],
        out_specs=pl.BlockSpec((None, 16, tn), lambda l, j: (l, 0, j)),
        compiler_params=_cp(("parallel", "parallel")),
        name="adaln",
    )(cond, ada_w, ada_b.reshape(DEPTH, 1, N_MOD * D))


def _ffn_body(h_ref, sh_ref, sc_ref, gt_ref, wup_ref, wdn_ref, fg_ref, o_ref, *, final):
    h = h_ref[...]
    xm = (_rms(h) * (1.0 + sc_ref[...]) + sh_ref[...]).astype(bf16)
    a = _dot(xm, wup_ref[:, :F])
    g = _dot(xm, wup_ref[:, F:])
    mid = (a * jax.nn.sigmoid(a) * g).astype(bf16)
    y = _dot(mid, wdn_ref[...])
    out = h + (0.5 * gt_ref[...]) * y
    o_ref[...] = _rms(out) * fg_ref[...] if final else out


def _ffn(h, shift, scale, gate, w_up, w_down, final_gain, n_rows, n_batch, final=False):
    mspec = _mod_spec(n_batch)
    return pl.pallas_call(
        functools.partial(_ffn_body, final=final),
        out_shape=jax.ShapeDtypeStruct((n_rows, D), f32),
        grid=(n_rows // TM,),
        in_specs=[pl.BlockSpec((TM, D), lambda i: (i, 0)), mspec, mspec, mspec,
                  _resident((D, 2 * F)), _resident((F, D)), pl.BlockSpec((1, D), lambda i: (0, 0))],
        out_specs=pl.BlockSpec((TM, D), lambda i: (i, 0)),
        compiler_params=_cp(("parallel",), VMEM_BIG),
        name="ffn",
    )(h, shift, scale, gate, w_up, w_down, final_gain)


PROJ_CHUNK = 2048


def _proj_body(h_ref, sh_ref, sc_ref, w_ref, wt_ref, o_ref, ot_ref):
    xm = (_rms(h_ref[...]) * (1.0 + sc_ref[...]) + sh_ref[...]).astype(bf16)
    for j in range(0, PW, PROJ_CHUNK):
        width = min(PROJ_CHUNK, PW - j)
        o_ref[:, j:j + width] = _dot(xm, w_ref[:, j:j + width]).astype(bf16)
    ot_ref[...] = _dot_t(wt_ref[...], xm).astype(bf16)


def _proj(h, shift, scale, w, w_t, n_rows, n_batch):
    return pl.pallas_call(
        _proj_body,
        out_shape=(jax.ShapeDtypeStruct((n_rows, PW), bf16), jax.ShapeDtypeStruct((PT_ROWS, n_rows), bf16)),
        grid=(n_rows // TM,),
        in_specs=[pl.BlockSpec((TM, D), lambda i: (i, 0)), _mod_spec(n_batch), _mod_spec(n_batch),
                  _resident((D, PW)), _resident((PT_ROWS, D))],
        out_specs=(pl.BlockSpec((TM, PW), lambda i: (i, 0)), pl.BlockSpec((PT_ROWS, TM), lambda i: (0, i))),
        compiler_params=_cp(("parallel",), VMEM_BIG),
        name="proj",
    )(h, shift, scale, w, w_t)


def _group_mean_sq(x, gmat, width):
    x2 = x * x
    hi = x2.astype(bf16)
    lo = (x2 - hi.astype(f32)).astype(bf16)
    return (_dot(hi, gmat) + _dot(lo, gmat)) * (1.0 / width)


def _swap_halves(x, half):
    n = x.shape[-1]
    lane = lax.broadcasted_iota(jnp.int32, x.shape, 1)
    first = (lane % (2 * half)) < half
    return jnp.where(first, pltpu.roll(x, n - half, 1), pltpu.roll(x, half, 1))


def _rope(x, cos, sin_signed, half):
    return x * cos + _swap_halves(x, half) * sin_signed


KCH = 256
VT_ROWS = 80


def _attend_key_major(get_k, get_vt, q_ts, n, dv, get_bias=None):
    nh, nc = len(q_ts), n // KCH
    tq = q_ts[0].shape[1]
    def scores(i, c):
        s = _dot(get_k(i, c), q_ts[i])
        return s if get_bias is None else s + get_bias(i, c)

    m = [jnp.full((1, tq), NEG, f32) for _ in range(nh)]
    o = [jnp.zeros((VT_ROWS, tq), f32) for _ in range(nh)]
    s_next = [scores(i, 0) for i in range(nh)]
    for c in range(nc):
        s_cur = s_next
        if c + 1 < nc:
            s_next = [scores(i, c + 1) for i in range(nh)]
        for i in range(nh):
            m_new = jnp.maximum(m[i], jnp.max(s_cur[i], axis=0, keepdims=True))
            e = jnp.exp2(s_cur[i] - m_new).astype(bf16)
            o[i] = o[i] * jnp.exp2(m[i] - m_new) + _dot(get_vt(i, c), e)
            m[i] = m_new
    return [oi[:dv, :] / oi[dv:dv + 1, :] for oi in o]


def _q_row_block(n_batch):
    nq = S // TQ
    return lambda b, qi: jnp.where(qi < nq, b * nq + qi, n_batch * nq + b)


def _swap_row_halves(x, half):
    parts = []
    for g in range(x.shape[0] // (2 * half)):
        parts += [x[g * 2 * half + half:(g + 1) * 2 * half, :], x[g * 2 * half:g * 2 * half + half, :]]
    return jnp.concatenate(parts, axis=0)


def _gqa_body(qt_ref, kl_ref, kc_ref, vl_ref, vc_ref, qcos_ref, qsin_ref, kcos_ref, ksin_ref,
              kg_ref, gm_ref, o_ref, k_s, vt_s, ot_s):
    qi = pl.program_id(1)
    nq = S // TQ

    @pl.when(qi == 0)
    def _():
        gmk = gm_ref[...]
        kl = kl_ref[...].astype(f32)
        kn = kl * lax.rsqrt(_group_mean_sq(kl, gmk, HEAD) + RMS_EPS) * kg_ref[...]
        kn = _rope(kn, kcos_ref[...], ksin_ref[...], 16).astype(bf16)
        kc = kc_ref[...].astype(f32)
        kcn = (kc * lax.rsqrt(_group_mean_sq(kc, gmk, HEAD) + RMS_EPS) * kg_ref[...]).astype(bf16)
        vl_t = vl_ref[...].astype(f32).T.astype(bf16)
        vc_t = vc_ref[...].astype(f32).T.astype(bf16)
        for g in range(GQA_KVH):
            k_s[g, :S, :] = kn[:, g * HEAD:(g + 1) * HEAD]
            k_s[g, S:, :] = kcn[:, g * HEAD:(g + 1) * HEAD]
            vt_s[g, :HEAD, :S] = vl_t[g * HEAD:(g + 1) * HEAD, :]
            vt_s[g, :HEAD, S:] = vc_t[g * HEAD:(g + 1) * HEAD, :]
            vt_s[g, HEAD:, :] = jnp.ones((VT_ROWS - HEAD, S + NCTX), bf16)

    q = qt_ref[...].astype(f32)
    q2 = q * q
    inv = [lax.rsqrt(jnp.sum(q2[h * HEAD:(h + 1) * HEAD, :], axis=0, keepdims=True) * (1.0 / HEAD) + RMS_EPS)
           for h in range(GQA_H)]
    qn = q * jnp.concatenate([jnp.broadcast_to(r, (HEAD, TQ)) for r in inv], axis=0)
    q_t = (qn * qcos_ref[...] + _swap_row_halves(qn, 16) * qsin_ref[...]).astype(bf16)

    def attend(lo):
        for g in range(GQA_KVH):
            heads = range(g * GQA_GROUP, (g + 1) * GQA_GROUP)
            outs = _attend_key_major(
                lambda i, c: k_s[g, lo + c * KCH:lo + (c + 1) * KCH, :],
                lambda i, c: vt_s[g, :, lo + c * KCH:lo + (c + 1) * KCH],
                [q_t[h * HEAD:(h + 1) * HEAD, :] for h in heads], S + NCTX - lo, HEAD)
            for h, o_t in zip(heads, outs):
                ot_s[h * HEAD:(h + 1) * HEAD, :] = o_t
        o_ref[...] = ot_s[...].T.astype(bf16)

    @pl.when(qi < nq)
    def _():
        attend(0)

    @pl.when(qi >= nq)
    def _():
        attend(S)


def _gqa(p, pt, tabs, k_gain, gmat, n_batch, with_ctx):
    nq = S // TQ
    n_rows = n_batch * S + (n_batch * NCTX if with_ctx else 0)
    qblk = _q_row_block(n_batch)
    ctx_blk = lambda b, qi: n_batch * nq + b
    qcos, qsin, kcos, ksin = tabs
    return pl.pallas_call(
        _gqa_body,
        out_shape=jax.ShapeDtypeStruct((n_rows, BW), bf16),
        grid=(n_batch, nq + (1 if with_ctx else 0)),
        in_specs=[
            pl.BlockSpec((BW, TQ), lambda b, qi: (R_GQ // BW, qblk(b, qi))),
            pl.BlockSpec((S, 128), lambda b, qi: (b, C_GK // 128)),
            pl.BlockSpec((NCTX, 128), lambda b, qi: (ctx_blk(b, qi), C_GK // 128)),
            pl.BlockSpec((S, 128), lambda b, qi: (b, C_GV // 128)),
            pl.BlockSpec((NCTX, 128), lambda b, qi: (ctx_blk(b, qi), C_GV // 128)),
            pl.BlockSpec((BW, TQ), lambda b, qi: (0, jnp.minimum(qi, nq))),
            pl.BlockSpec((BW, TQ), lambda b, qi: (0, jnp.minimum(qi, nq))),
            pl.BlockSpec((S, 128), lambda b, qi: (0, 0)),
            pl.BlockSpec((S, 128), lambda b, qi: (0, 0)),
            pl.BlockSpec((1, 128), lambda b, qi: (0, 0)),
            pl.BlockSpec((2 * HEAD, 2 * HEAD), lambda b, qi: (0, 0)),
        ],
        out_specs=pl.BlockSpec((TQ, BW), lambda b, qi: (qblk(b, qi), 0)),
        scratch_shapes=[pltpu.VMEM((GQA_KVH, S + NCTX, HEAD), bf16),
                        pltpu.VMEM((GQA_KVH, VT_ROWS, S + NCTX), bf16),
                        pltpu.VMEM((BW, TQ), f32)],
        compiler_params=_cp(("parallel", "arbitrary"), VMEM_BIG),
        name="gqa",
    )(pt, p, p, p, p, qcos, qsin, kcos, ksin, k_gain, gmat)


def _mla_body(qat_ref, kval_ref, kvac_ref, krl_ref, krc_ref, qcos_ref, qsin_ref, kcos_ref, ksin_ref,
              qg_ref, kvg_ref, wqt_ref, wkv_ref, o_ref, k_s, vt_s, ot_s):
    qi = pl.program_id(1)
    nq = S // TQ
    slot = MLA_SLOT

    @pl.when(qi == 0)
    def _():
        lane = lax.broadcasted_iota(jnp.int32, (1, 2 * slot), 1)
        ones_hi = jnp.where(lane % slot >= MLA_V, 1.0, 0.0)

        def kv(ref, kr, lo, n):
            a = (_rms(ref[...].astype(f32)) * kvg_ref[...]).astype(bf16)
            for hp in range(MLA_H // 2):
                kk = _dot(a, wkv_ref[:, hp * 2 * slot:(hp + 1) * 2 * slot])
                vv = _dot(a, wkv_ref[:, (MLA_H + hp * 2) * slot:(MLA_H + hp * 2 + 2) * slot])
                vv_t = (vv + ones_hi).T.astype(bf16)
                for i in range(2):
                    k_s[2 * hp + i, lo:lo + n, :] = (kk[:, i * slot:(i + 1) * slot] + kr).astype(bf16)
                    vt_s[2 * hp + i, :, lo:lo + n] = vv_t[i * slot:i * slot + VT_ROWS, :]

        kv(kval_ref, _rope(krl_ref[...].astype(f32), kcos_ref[...], ksin_ref[...], 8), 0, S)
        kv(kvac_ref, krc_ref[...].astype(f32), S, NCTX)

    qa = qat_ref[...].astype(f32)
    qan = qa * lax.rsqrt(jnp.mean(qa * qa, axis=0, keepdims=True) + RMS_EPS)
    qh = _dot(wqt_ref[...], (qan * qg_ref[...]).astype(bf16))
    q_t = (qh * qcos_ref[...] + _swap_row_halves(qh, 8) * qsin_ref[...]).astype(bf16)

    def attend(lo):
        for h0 in range(0, MLA_H, MLA_LOCKSTEP):
            heads = range(h0, h0 + MLA_LOCKSTEP)
            outs = _attend_key_major(
                lambda i, c: k_s[h0 + i, lo + c * KCH:lo + (c + 1) * KCH, :],
                lambda i, c: vt_s[h0 + i, :, lo + c * KCH:lo + (c + 1) * KCH],
                [q_t[h * slot:(h + 1) * slot, :] for h in heads], S + NCTX - lo, MLA_V)
            for h, o_t in zip(heads, outs):
                ot_s[h * MLA_V:(h + 1) * MLA_V, :] = o_t
        o_ref[...] = ot_s[...].T.astype(bf16)

    @pl.when(qi < nq)
    def _():
        attend(0)

    @pl.when(qi >= nq)
    def _():
        attend(S)


def _mla(p, pt, tabs, q_gain, kv_gain, wq_t, wkv, n_batch, with_ctx):
    nq = S // TQ
    n_rows = n_batch * S + (n_batch * NCTX if with_ctx else 0)
    qblk = _q_row_block(n_batch)
    ctx_blk = lambda b, qi: n_batch * nq + b
    qcos, qsin, kcos, ksin = tabs
    rw = MLA_H * MLA_SLOT
    return pl.pallas_call(
        _mla_body,
        out_shape=jax.ShapeDtypeStruct((n_rows, BW), bf16),
        grid=(n_batch, nq + (1 if with_ctx else 0)),
        in_specs=[
            pl.BlockSpec((MLA_QR, TQ), lambda b, qi: (R_MQA // MLA_QR, qblk(b, qi))),
            pl.BlockSpec((S, MLA_KVR), lambda b, qi: (b, C_MKVA // MLA_KVR)),
            pl.BlockSpec((NCTX, MLA_KVR), lambda b, qi: (ctx_blk(b, qi), C_MKVA // MLA_KVR)),
            pl.BlockSpec((S, 128), lambda b, qi: (b, C_MKR // 128)),
            pl.BlockSpec((NCTX, 128), lambda b, qi: (ctx_blk(b, qi), C_MKR // 128)),
            pl.BlockSpec((rw, TQ), lambda b, qi: (0, jnp.minimum(qi, nq))),
            pl.BlockSpec((rw, TQ), lambda b, qi: (0, jnp.minimum(qi, nq))),
            pl.BlockSpec((S, 128), lambda b, qi: (0, 0)),
            pl.BlockSpec((S, 128), lambda b, qi: (0, 0)),
            pl.BlockSpec((MLA_QR, TQ), lambda b, qi: (0, 0)),
            pl.BlockSpec((1, MLA_KVR), lambda b, qi: (0, 0)),
            pl.BlockSpec((MLA_H * MLA_SLOT, MLA_QR), lambda b, qi: (0, 0)),
            pl.BlockSpec((MLA_KVR, 2 * MLA_H * MLA_SLOT), lambda b, qi: (0, 0)),
        ],
        out_specs=pl.BlockSpec((TQ, BW), lambda b, qi: (qblk(b, qi), 0)),
        scratch_shapes=[pltpu.VMEM((MLA_H, S + NCTX, MLA_SLOT), bf16),
                        pltpu.VMEM((MLA_H, VT_ROWS, S + NCTX), bf16),
                        pltpu.VMEM((BW, TQ), f32)],
        compiler_params=_cp(("parallel", "arbitrary"), VMEM_BIG),
        name="mla",
    )(pt, p, p, p, p, qcos, qsin, kcos, ksin, q_gain, kv_gain, wq_t, wkv)


NA_LOCKSTEP = 4


def _na_body(qt_ref, k0_ref, k1_ref, k2_ref, kc_ref, vt0_ref, vt1_ref, vt2_ref, vtc_ref, bias_ref, o_ref, ot_s):
    t = pl.program_id(0)
    nt = S // TQ
    row = lax.broadcasted_iota(jnp.int32, (2 * HEAD, TQ), 0)
    ones = jnp.ones((VT_ROWS - HEAD, KCH), bf16)

    def run(k_refs, vt_refs, with_bias):
        for h0 in range(0, NA_H, NA_LOCKSTEP):
            q_ts = []
            for pair in range(h0 // 2, (h0 + NA_LOCKSTEP) // 2):
                qp = qt_ref[pair * 2 * HEAD:(pair + 1) * 2 * HEAD, :]
                q_ts += [jnp.where(row < HEAD, qp, jnp.zeros_like(qp)), jnp.where(row >= HEAD, qp, jnp.zeros_like(qp))]
            outs = _attend_key_major(
                lambda i, c: k_refs[c][:, ((h0 + i) // 2) * 2 * HEAD:((h0 + i) // 2 + 1) * 2 * HEAD],
                lambda i, c: jnp.concatenate([vt_refs[c][(h0 + i) * HEAD:(h0 + i + 1) * HEAD, :], ones], axis=0),
                q_ts, len(k_refs) * KCH, HEAD,
                (lambda i, c: bias_ref[h0 + i, c * KCH:(c + 1) * KCH, :]) if with_bias else None)
            for i, o_t in enumerate(outs):
                ot_s[(h0 + i) * HEAD:(h0 + i + 1) * HEAD, :] = o_t
        o_ref[...] = ot_s[...].T.astype(bf16)

    @pl.when(t < nt)
    def _():
        run([k0_ref, k1_ref, k2_ref, kc_ref], [vt0_ref, vt1_ref, vt2_ref, vtc_ref], True)

    @pl.when(t >= nt)
    def _():
        run([kc_ref], [vtc_ref], False)


def _na(p, pt, bias, n_batch, with_ctx):
    nt = S // TQ
    n_rows = n_batch * S + (n_batch * NCTX if with_ctx else 0)
    qblk = lambda t, b: jnp.where(t < nt, b * nt + t, n_batch * nt + b)
    ctx_blk = lambda t, b: n_batch * nt + b
    centre = lambda t: jnp.clip(t, 1, nt - 2)
    variant = lambda t: jnp.where(t == 0, 0, jnp.where(t >= nt - 1, 2, 1))

    def kwin(d):
        return pl.BlockSpec((TQ, BW), lambda t, b: (b * nt + centre(t) + d, C_NK // BW))

    def vwin(d):
        return pl.BlockSpec((BW, TQ), lambda t, b: (R_NV // BW, b * nt + centre(t) + d))

    return pl.pallas_call(
        _na_body,
        out_shape=jax.ShapeDtypeStruct((n_rows, BW), bf16),
        grid=(nt + (1 if with_ctx else 0), n_batch),
        in_specs=[
            pl.BlockSpec((BW, TQ), lambda t, b: (R_NQ // BW, qblk(t, b))),
            kwin(-1), kwin(0), kwin(1),
            pl.BlockSpec((NCTX, BW), lambda t, b: (ctx_blk(t, b), C_NK // BW)),
            vwin(-1), vwin(0), vwin(1),
            pl.BlockSpec((BW, NCTX), lambda t, b: (R_NV // BW, ctx_blk(t, b))),
            pl.BlockSpec((None, NA_H, 4 * TQ, TQ), lambda t, b: (variant(t), 0, 0, 0)),
        ],
        out_specs=pl.BlockSpec((TQ, BW), lambda t, b: (qblk(t, b), 0)),
        scratch_shapes=[pltpu.VMEM((BW, TQ), f32)],
        compiler_params=_cp(("arbitrary", "arbitrary"), VMEM_BIG),
        name="na",
    )(pt, p, p, p, p, pt, pt, pt, pt, bias)


def _na_bias(rpb):
    rows_per_tile = TQ // GRID_W
    n_rows = S // GRID_W
    kc = np.arange(GRID_W)[:, None]
    qc = np.arange(GRID_W)[None, :]
    col_start = np.clip(qc - NA_COLS // 2, 0, GRID_W - NA_COLS)
    col_ok = (kc >= col_start) & (kc < col_start + NA_COLS)
    cidx = np.clip(kc - qc + NA_COLS - 1, 0, 2 * NA_COLS - 2)
    onehot = (np.arange(2 * NA_COLS - 1)[:, None, None] == cidx[None]).astype(np.float32)
    blocks = jnp.einsum('lhrc,ckq->lhrkq', rpb.astype(f32), jnp.asarray(onehot), precision=lax.Precision.HIGHEST)
    blocks = jnp.where(jnp.asarray(col_ok), blocks * LOG2E, NEG)
    n_off = 2 * NA_ROWS - 1
    fill = lambda v: jnp.full((DEPTH, NA_H, 1, GRID_W, GRID_W), v, f32)
    blocks = jnp.concatenate([blocks, fill(NEG), fill(0.0)], axis=2)
    key_blocks = 3 * rows_per_tile + NCTX // GRID_W
    select = np.zeros((3, key_blocks, rows_per_tile, n_off + 2), np.float32)
    for v, t in enumerate((0, 1, S // TQ - 1)):
        centre = min(max(t, 1), S // TQ - 2)
        for qoff in range(rows_per_tile):
            r = rows_per_tile * t + qoff
            start = min(max(r - NA_ROWS // 2, 0), n_rows - NA_ROWS)
            for kb in range(key_blocks):
                krow = rows_per_tile * (centre - 1) + kb
                if kb >= 3 * rows_per_tile:
                    j = n_off + 1
                elif start <= krow < start + NA_ROWS:
                    j = krow - r + NA_ROWS - 1
                else:
                    j = n_off
                select[v, kb, qoff, j] = 1.0
    out = jnp.einsum('lhjkq,vbaj->lvhbkaq', blocks, jnp.asarray(select), precision=lax.Precision.HIGHEST)
    return out.reshape(DEPTH, 3, NA_H, key_blocks * GRID_W, TQ)


def _dft_tables(length):
    n = 2 * length
    blk = 32
    f = jnp.arange(length, dtype=jnp.int32)[:, None]

    def cos_sin(step, count):
        ang = ((f * (step * jnp.arange(count, dtype=jnp.int32))[None, :]) % n).astype(f32) * (2.0 * math.pi / n)
        return jnp.cos(ang)[:, :, None], jnp.sin(ang)[:, :, None]

    c1, s1 = cos_sin(blk, length // blk)
    c0, s0 = (t.reshape(length, 1, blk) for t in cos_sin(1, blk))
    cos_t = (c1 * c0 - s1 * s0).reshape(length, length)
    sin_t = (s1 * c0 + c1 * s0).reshape(length, length)
    return cos_t.astype(bf16), sin_t.astype(bf16)


def _filter_features(length):
    t = jnp.linspace(0.0, 1.0, length, dtype=f32)[:, None]
    w = (2.0 * math.pi / length) * jnp.arange(length, dtype=f32)[:, None]
    fb = jnp.linspace(1e-4, HY_BANDS - 1, HY_BANDS, dtype=f32)[None, :]
    z = jnp.concatenate([t, jnp.cos(fb * w), -jnp.sin(fb * w)], axis=-1)
    deltas = jnp.abs(jnp.linspace(HY_MIN_DECAY, HY_MAX_DECAY, HY_W, dtype=f32))
    dec = jnp.exp(-t * deltas)
    z = jnp.pad(z, ((0, 0), (0, 64 - HY_EMB)))
    z_rev = jnp.concatenate([z[:1], z[:0:-1]], axis=0)
    dec_rev = jnp.concatenate([jnp.zeros((1, HY_W), f32), dec[:0:-1]], axis=0)
    return z, z_rev, dec, dec_rev


HY_CW = 256
HY_FC = 512


def _alternating(shape):
    row = lax.broadcasted_iota(jnp.int32, shape, 0)
    return jnp.where(row % 2 == 0, 1.0, -1.0)


def _hyfilt_body(z_ref, zr_ref, dec_ref, decr_ref, w1_ref, b_ref, fr_ref, wm_ref, wof_ref, wob_ref, a_ref, sn_ref,
                 kp_ref, kq_ref, hf_s, hb_s):
    length = z_ref.shape[0]
    fc = min(HY_FC, length)

    def hidden(z):
        hdn = jnp.sin(fr_ref[0:1, :] * (_dot(z, w1_ref[...]) + b_ref[0:1, :]))
        hdn = jnp.sin(fr_ref[1:2, :] * (_dot(hdn, wm_ref[0]) + b_ref[1:2, :]))
        return jnp.sin(fr_ref[2:3, :] * (_dot(hdn, wm_ref[1]) + b_ref[2:3, :]))

    @pl.when(pl.program_id(1) == 0)
    def _():
        hf_s[...] = hidden(z_ref[...])
        hb_s[...] = hidden(zr_ref[...])

    ff = _dot(hf_s[...], wof_ref[...]) * dec_ref[...]
    fb = _dot(hb_s[...], wob_ref[...]) * decr_ref[...]
    norm = jnp.sum(jnp.abs(ff), axis=0, keepdims=True) + jnp.sum(jnp.abs(fb), axis=0, keepdims=True)
    ff = ff / norm
    fb = fb / norm
    inv_n = 1.0 / (2 * length)
    nyq = jnp.sum(_alternating((length, HY_CW)) * (ff + fb), axis=0, keepdims=True) * inv_n
    ff_b = ff.astype(bf16)
    fb_b = fb.astype(bf16)
    sign = _alternating((fc, HY_CW))
    for c in range(length // fc):
        rows = slice(c * fc, (c + 1) * fc)
        a = a_ref[rows, :]
        sn = sn_ref[rows, :]
        row = lax.broadcasted_iota(jnp.int32, (fc, HY_CW), 0) + c * fc
        wgt = jnp.where(row == 0, inv_n, 2.0 * inv_n)
        kp_ref[rows, :] = (_dot(a, ff_b) + sign * _dot(a, fb_b)) * wgt
        kq = (_dot(sn, ff_b) + sign * _dot(sn, fb_b)) * wgt
        kq_ref[rows, :] = jnp.where(row == 0, nyq, kq)


def _hy_filter(length, feats, w1, fb, ffreq, wmid, wout, a, sn):
    z, z_rev, dec, dec_rev = feats
    nj = HY_W // HY_CW
    full = lambda shape: pl.BlockSpec(shape, lambda l, j: (0,) * len(shape))
    chan = lambda shape: pl.BlockSpec(shape, lambda l, j: (0, j))
    per_layer = lambda shape: pl.BlockSpec((None,) + shape, lambda l, j: (l,) + (0,) * len(shape))
    out_spec = pl.BlockSpec((None, length, HY_CW), lambda l, j: (l, 0, j))
    return pl.pallas_call(
        _hyfilt_body,
        out_shape=(jax.ShapeDtypeStruct((DEPTH, length, HY_W), f32),
                   jax.ShapeDtypeStruct((DEPTH, length, HY_W), f32)),
        grid=(DEPTH, nj),
        in_specs=[full((length, 64)), full((length, 64)), chan((length, HY_CW)), chan((length, HY_CW)),
                  per_layer((64, HY_HID)), per_layer((3, HY_HID)), per_layer((3, HY_HID)),
                  per_layer((2, HY_HID, HY_HID)),
                  pl.BlockSpec((None, HY_HID, HY_CW), lambda l, j: (l, 0, j)),
                  pl.BlockSpec((None, HY_HID, HY_CW), lambda l, j: (l, 0, nj + j)),
                  full((length, length)), full((length, length))],
        out_specs=(out_spec, out_spec),
        scratch_shapes=[pltpu.VMEM((length, HY_HID), f32), pltpu.VMEM((length, HY_HID), f32)],
        compiler_params=_cp(("parallel", "arbitrary"), VMEM_BIG),
        name="hyena_filter",
    )(z, z_rev, dec, dec_rev, w1, fb, ffreq, wmid, wout, wout, a, sn)


def _hyconv_body(x0_ref, x1_ref, v_ref, cw_ref, cb_ref, skip_ref, a_ref, sn_ref, kp_ref, kq_ref, o_ref):
    length = x0_ref.shape[0]
    fc = min(HY_FC, length)
    row = lax.broadcasted_iota(jnp.int32, (length, HY_CW), 0)

    def short_conv(u_ref, part):
        u = u_ref[...].astype(f32)
        prev = jnp.where(row == 0, 0.0, pltpu.roll(u, 1, 0))
        nxt = jnp.where(row == length - 1, 0.0, pltpu.roll(u, length - 1, 0))
        w = cw_ref[part]
        return prev * w[0:1, :] + u * w[1:2, :] + nxt * w[2:3, :] + cb_ref[part]

    x0 = short_conv(x0_ref, 0)
    x1 = short_conv(x1_ref, 1)
    vv = short_conv(v_ref, 2) * x1
    vb = vv.astype(bf16)
    alt = _alternating((length, HY_CW))
    y = alt * (jnp.sum(alt * vv, axis=0, keepdims=True) * kq_ref[0:1, :])
    for c in range(length // fc):
        rows = slice(c * fc, (c + 1) * fc)
        pc = _dot(a_ref[rows, :], vb)
        qs = _dot(sn_ref[rows, :], vb)
        kp = kp_ref[rows, :]
        kq = kq_ref[rows, :]
        yp = (pc * kp - qs * kq).astype(bf16)
        yq = (pc * kq + qs * kp).astype(bf16)
        y = y + _dot(a_ref[:, rows], yp) + _dot(sn_ref[:, rows], yq)
    o_ref[...] = ((y + vv * skip_ref[...]) * x0).astype(bf16)


def _hy_conv(p, length, row_blk0, n_seq, conv_w, conv_b, skip, tabs, kp, kq):
    a, sn = tabs
    nj = HY_W // HY_CW
    c0 = C_HY // HY_CW

    def part(k):
        return pl.BlockSpec((length, HY_CW), lambda j, b: (row_blk0 + b, c0 + k * nj + j))

    chan3 = lambda shape: pl.BlockSpec(shape, lambda j, b: (0, 0, j))
    return pl.pallas_call(
        _hyconv_body,
        out_shape=jax.ShapeDtypeStruct((n_seq * length, HY_W), bf16),
        grid=(nj, n_seq),
        in_specs=[part(0), part(1), part(2),
                  chan3((3, 3, HY_CW)), chan3((3, 1, HY_CW)),
                  pl.BlockSpec((1, HY_CW), lambda j, b: (0, j)),
                  _resident((length, length)), _resident((length, length)),
                  pl.BlockSpec((length, HY_CW), lambda j, b: (0, j)),
                  pl.BlockSpec((length, HY_CW), lambda j, b: (0, j))],
        out_specs=pl.BlockSpec((length, HY_CW), lambda j, b: (b, j)),
        compiler_params=_cp(("parallel", "parallel"), VMEM_BIG),
        name="hyena_conv",
    )(p, p, p, conv_w, conv_b, skip, a, sn, kp, kq)


def _merge_body(og_ref, oh_ref, on_ref, om_ref, gate_ref, wb_ref, wo_ref, h_ref, m_ref, o_ref):
    merged = None
    for i, o in enumerate((og_ref, oh_ref, on_ref, om_ref)):
        g = jax.nn.sigmoid(gate_ref[:, i * D:(i + 1) * D].astype(f32))
        term = g * _dot(o[...], wb_ref[i])
        merged = term if merged is None else merged + term
    mix = _dot(merged.astype(bf16), wo_ref[...])
    o_ref[...] = h_ref[...] + m_ref[...] * mix


def _merge(outs, p, w_branch, w_out, h, mgate, n_rows, n_batch):
    ospec = pl.BlockSpec((TM, BW), lambda i: (i, 0))
    return pl.pallas_call(
        _merge_body,
        out_shape=jax.ShapeDtypeStruct((n_rows, D), f32),
        grid=(n_rows // TM,),
        in_specs=[ospec, ospec, ospec, ospec,
                  pl.BlockSpec((TM, 4 * D), lambda i: (i, C_GATE // (4 * D))),
                  _resident((4, BW, D)), _resident((D, D)),
                  pl.BlockSpec((TM, D), lambda i: (i, 0)), _mod_spec(n_batch)],
        out_specs=pl.BlockSpec((TM, D), lambda i: (i, 0)),
        compiler_params=_cp(("parallel",), VMEM_BIG),
        name="merge",
    )(*outs, p, w_branch, w_out, h, mgate)


def _rope_table(half, n_heads):
    freqs = ROPE_THETA ** (-jnp.arange(half, dtype=f32) / half)
    pos = jnp.arange(S)
    ar = (pos // GRID_W).astype(f32)[:, None] * freqs
    ac = (pos % GRID_W).astype(f32)[:, None] * freqs
    cos_h = jnp.concatenate([jnp.cos(ar), jnp.cos(ar), jnp.cos(ac), jnp.cos(ac)], axis=-1)
    sin_h = jnp.concatenate([-jnp.sin(ar), jnp.sin(ar), -jnp.sin(ac), jnp.sin(ac)], axis=-1)
    return jnp.tile(cos_h, (1, n_heads)), jnp.tile(sin_h, (1, n_heads))


def _with_identity_rows(cos, sin, n):
    return (jnp.concatenate([cos, jnp.ones((n, cos.shape[1]), f32)], axis=0),
            jnp.concatenate([sin, jnp.zeros((n, sin.shape[1]), f32)], axis=0))


def _gqa_q_tables(tab, q_gain):
    cos, sin = tab
    g = jnp.tile(q_gain, GQA_H)
    g_partner = g.reshape(-1, 2, 16)[:, ::-1, :].reshape(-1)
    scale = HEAD ** -0.5 * LOG2E
    return (cos.T * (g * scale)[:, None], sin.T * (g_partner * scale)[:, None])


def _mla_rope_table():
    cos, sin = _rope_table(8, 1)
    pad = MLA_SLOT - MLA_NOPE - MLA_ROPE
    return (jnp.concatenate([jnp.ones((S, MLA_NOPE), f32), cos, jnp.ones((S, pad), f32)], axis=1),
            jnp.concatenate([jnp.zeros((S, MLA_NOPE), f32), sin, jnp.zeros((S, pad), f32)], axis=1))


def _pack_w_in(w_in):
    z = lambda n: jnp.zeros(w_in.shape[:-1] + (n,), w_in.dtype)
    sl = lambda a, b: w_in[..., a:b]
    return jnp.concatenate([
        sl(4512, 8608),
        sl(768, 2304),
        sl(2816, 3328),
        sl(4224, 4480),
        sl(512, 768),
        z(MLA_NOPE), sl(4480, 4512), z(MLA_SLOT - MLA_NOPE - MLA_ROPE),
    ], axis=-1).astype(bf16)


def _pack_w_in_t(w_in):
    sl = lambda a, b: jnp.swapaxes(w_in[..., a:b], -1, -2)
    return jnp.concatenate([
        sl(2304, 2816) * (HEAD ** -0.5 * LOG2E),
        sl(3328, 3840),
        sl(0, 512),
        sl(3840, 4224),
    ], axis=-2).astype(bf16)


def _pack_wq(wq_b):
    w = wq_b.reshape(DEPTH, MLA_QR, MLA_H, MLA_NOPE + MLA_ROPE)
    w = jnp.pad(w, ((0, 0), (0, 0), (0, 0), (0, MLA_SLOT - MLA_NOPE - MLA_ROPE)))
    return jnp.swapaxes(w.reshape(DEPTH, MLA_QR, MLA_H * MLA_SLOT), 1, 2).astype(bf16)


def _pack_wkv(wkv_b):
    w = wkv_b.reshape(DEPTH, MLA_KVR, MLA_H, MLA_NOPE + MLA_V)
    slots = lambda a: jnp.pad(a, ((0, 0), (0, 0), (0, 0), (0, MLA_SLOT - a.shape[-1]))).reshape(DEPTH, MLA_KVR, -1)
    return jnp.concatenate([slots(w[..., :MLA_NOPE]), slots(w[..., MLA_NOPE:])], axis=-1).astype(bf16)


def kernel(x, c, ctx, c_ctx, ada_w, ada_b, ffn1_up, ffn1_down, w_in, gqa_q_norm, gqa_k_norm, hy_conv_w, hy_conv_b, hy_f_w1, hy_f_b, hy_f_freq, hy_f_w_mid, hy_f_w_out, hy_skip, na_rpb, mla_q_norm, mla_kv_norm, mla_wq_b, mla_wkv_b, w_branch, w_out, ffn2_up, ffn2_down, final_norm):
    nb = x.shape[0]
    rows_lat = nb * S
    rows_all = rows_lat + nb * NCTX

    cond = jnp.concatenate([c, c_ctx[None], jnp.zeros((16 - nb - 1, D), f32)], axis=0)
    mod = _adaln(cond, ada_w, ada_b)[:, :nb + 1].reshape(DEPTH, nb + 1, N_MOD, 1, D)
    mods = lambda l, k: mod[l, :, k]

    w1u, w1d = ffn1_up.astype(bf16), ffn1_down.astype(bf16)
    w2u, w2d = ffn2_up.astype(bf16), ffn2_down.astype(bf16)
    w_in_p, w_in_t = _pack_w_in(w_in), _pack_w_in_t(w_in)
    wq_p, wkv_p = _pack_wq(mla_wq_b), _pack_wkv(mla_wkv_b)
    wb, wo = w_branch.astype(bf16), w_out.astype(bf16)

    gq_tab = _with_identity_rows(*_rope_table(16, GQA_H), NCTX)
    gk_tab = _rope_table(16, GQA_KVH)
    mk_tab = _mla_rope_table()
    mq_tab = _with_identity_rows(jnp.tile(mk_tab[0], (1, MLA_H)), jnp.tile(mk_tab[1], (1, MLA_H)), NCTX)
    mq_tab = tuple(t.T * (MLA_SCALE * LOG2E) for t in mq_tab)
    lane = jnp.arange(GQA_KVH * HEAD)
    gmat = (lane[:, None] // HEAD == lane[None, :] // HEAD).astype(bf16)
    na_bias = _na_bias(na_rpb)

    hy_w1 = jnp.pad(hy_f_w1, ((0, 0), (0, 64 - HY_EMB), (0, 0)))
    dft_lat, dft_ctx = _dft_tables(S), _dft_tables(NCTX)
    kp_lat, kq_lat = _hy_filter(S, _filter_features(S), hy_w1, hy_f_b, hy_f_freq, hy_f_w_mid, hy_f_w_out,
                                dft_lat[0], dft_lat[1])
    kp_ctx, kq_ctx = _hy_filter(NCTX, _filter_features(NCTX), hy_w1, hy_f_b, hy_f_freq, hy_f_w_mid, hy_f_w_out,
                                dft_ctx[0], dft_ctx[1])
    conv_w = hy_conv_w.reshape(DEPTH, 3, 3, HY_W).transpose(0, 2, 1, 3)
    conv_b = hy_conv_b.reshape(DEPTH, 3, 1, HY_W)

    h = jnp.concatenate([x.reshape(rows_lat, D), ctx.reshape(nb * NCTX, D)], axis=0)
    for l in range(DEPTH):
        with_ctx = l < DEPTH - 1
        n_out = rows_all if with_ctx else rows_lat
        h = _ffn(h, mods(l, 0), mods(l, 1), mods(l, 2), w1u[l], w1d[l], final_norm[None], rows_all, nb)
        p, pt = _proj(h, mods(l, 3), mods(l, 4), w_in_p[l], w_in_t[l], rows_all, nb)
        o_gqa = _gqa(p, pt, _gqa_q_tables(gq_tab, gqa_q_norm[l]) + gk_tab, jnp.tile(gqa_k_norm[l], GQA_KVH)[None],
                     gmat, nb, with_ctx)
        o_hy = _hy_conv(p, S, 0, nb, conv_w[l], conv_b[l], hy_skip[l][None], dft_lat, kp_lat[l], kq_lat[l])
        if with_ctx:
            o_hy_c = _hy_conv(p, NCTX, rows_lat // NCTX, nb, conv_w[l], conv_b[l], hy_skip[l][None], dft_ctx,
                              kp_ctx[l], kq_ctx[l])
            o_hy = jnp.concatenate([o_hy, o_hy_c], axis=0)
        o_na = _na(p, pt, na_bias[l], nb, with_ctx)
        o_mla = _mla(p, pt, mq_tab + mk_tab, jnp.broadcast_to(mla_q_norm[l][:, None], (MLA_QR, TQ)),
                     mla_kv_norm[l][None], wq_p[l], wkv_p[l], nb, with_ctx)
        h = _merge((o_gqa, o_hy, o_na, o_mla), p, wb[l], wo[l], h, mods(l, 5), n_out, nb)
        h = _ffn(h, mods(l, 6), mods(l, 7), mods(l, 8), w2u[l], w2d[l], final_norm[None], n_out, nb,
                 final=not with_ctx)
    return h.reshape(nb, S, D)
```
